```python
import math
import jax, jax.numpy as jnp
from jax import lax
import numpy as np

D_MODEL = 1024
BATCH = 2
SEQ = 8192
DEPTH = 4
DEC_BATCH = 128
DEC_SEQ = 1
PAST_LEN = 2048
PAGE_SIZE = 128

N_META = 16
H_RET = 6
D_RET = 64
H_DSA = 5
D_DSA = 64
H_IDX = 8
D_IDX = 64
TOPK_MAX = 256
H_DIFF = 5
D_DIFF = 32
V_DIFF = 64
MIX = H_RET * D_RET + H_DSA * D_DSA + H_DIFF * V_DIFF
D_FF = 2816
CONV_W = 3
ROPE_THETA = 500000.0
RET_THETA = 10000.0
ROT_FRAC = 4
Q_BLOCK = 128
RET_CHUNK = 128
NEG = -1e30
EPS = 1e-6
PROJ_SIZES = (H_RET * D_RET,) * 4 + (H_DSA * D_DSA,) * 3 + (H_IDX * D_IDX, D_IDX, H_IDX) + (2 * H_DIFF * D_DIFF, 2 * H_DIFF * D_DIFF, H_DIFF * V_DIFF)
IN_WIDTH = sum(PROJ_SIZES)

kernel_name = 'hymba_ret_dsa_diff_step'

F32 = jnp.float32


def rmsnorm(x, g):
    xf = x.astype(F32)
    y = xf * lax.rsqrt(jnp.mean(xf * xf, axis=-1, keepdims=True) + EPS)
    return (y * g.astype(F32)).astype(x.dtype)


def rope(x, pos, rot_dim, theta):
    half = rot_dim // 2
    inv = jnp.exp(-math.log(theta) * jnp.arange(half, dtype=F32) / half)
    ang = pos[:, None] * inv[None, :]
    cos = jnp.cos(ang)[None, :, None, :]
    sin = jnp.sin(ang)[None, :, None, :]
    xf = x.astype(F32)
    x1, x2 = xf[..., :half], xf[..., half:rot_dim]
    out = jnp.concatenate([x1 * cos - x2 * sin, x2 * cos + x1 * sin, xf[..., rot_dim:]], axis=-1)
    return out.astype(x.dtype)


def retention_decay():
    return jnp.log(1.0 - 2.0 ** (-5.0 - jnp.arange(H_RET, dtype=F32)))


def retention_chunk(S, q, k, v, lg):
    c = q.shape[1]
    q, k, v, S = q.astype(F32), k.astype(F32), v.astype(F32), S.astype(F32)
    i = jnp.arange(c, dtype=F32)
    rel = i[:, None] - i[None, :]
    dmat = jnp.where(rel >= 0, jnp.exp(jnp.maximum(rel, 0.0)[None] * lg[:, None, None]), 0.0)
    inner = jnp.einsum('nihd,njhd->nhij', q, k) * dmat[None]
    o = jnp.einsum('nhij,njhe->nihe', inner, v)
    o = o + jnp.einsum('nihd,nhde->nihe', q, S) * jnp.exp((i[:, None] + 1.0) * lg[None, :])[None, :, :, None]
    kw = k * jnp.exp((c - 1.0 - i)[:, None] * lg[None, :])[None, :, :, None]
    S_new = jnp.exp(c * lg)[None, :, None, None] * S + jnp.einsum('njhd,njhe->nhde', kw, v)
    return o, S_new


def retention_prompt(q, k, v, lg):
    N, T = q.shape[:2]
    S0 = jnp.zeros((N, H_RET, D_RET, D_RET), F32)
    o_meta, S1 = retention_chunk(S0, q[:, :N_META], k[:, :N_META], v[:, :N_META], lg)
    n_c = (T - N_META) // RET_CHUNK

    def to_chunks(a):
        return jnp.moveaxis(a[:, N_META:].reshape(N, n_c, RET_CHUNK, *a.shape[2:]), 1, 0)

    def step(S, qkv):
        o, S = retention_chunk(S, qkv[0], qkv[1], qkv[2], lg)
        return S, o

    S_fin, o_c = lax.scan(step, S1, (to_chunks(q), to_chunks(k), to_chunks(v)))
    o_c = jnp.moveaxis(o_c, 0, 1).reshape(N, T - N_META, H_RET, D_RET)
    return jnp.concatenate([o_meta, o_c], axis=1), S_fin


gather_rows = jax.vmap(lambda rows, idx: rows[idx])


def indexer_scores(iq, iw, kidx, allowed):
    r = jax.nn.relu(jnp.einsum('nqhd,nkd->nqhk', iq.astype(F32), kidx.astype(F32)))
    scores = jnp.einsum('nqh,nqhk->nqk', iw.astype(F32), r)
    return jnp.where(allowed[None], scores, NEG)


def sparse_attend(q, k_sel, v_sel, valid):
    s = jnp.einsum('nqhd,nqkhd->nqhk', q, k_sel).astype(F32) * D_DSA ** -0.5
    s = jnp.where(valid[:, :, None, :], s, NEG)
    p = jax.nn.softmax(s, axis=-1)
    return jnp.einsum('nqhk,nqkhd->nqhd', p.astype(v_sel.dtype), v_sel)


def diff_attend(q, k, v, allowed, lam):
    s = jnp.einsum('nqhcd,nkhcd->nchqk', q, k).astype(F32) * D_DIFF ** -0.5
    s = jnp.where(allowed[None, None, None], s, NEG)
    p = jax.nn.softmax(s, axis=-1)
    a = p[:, 0] - lam * p[:, 1]
    return jnp.einsum('nhqk,nkhe->nqhe', a.astype(v.dtype), v)


def query_blocks(fn, T, *qs):
    n_blk = -(-T // Q_BLOCK)
    pad = n_blk * Q_BLOCK - T

    def blk(a):
        a = jnp.pad(a, [(0, 0), (0, pad)] + [(0, 0)] * (a.ndim - 2))
        return jnp.moveaxis(a.reshape(a.shape[0], n_blk, Q_BLOCK, *a.shape[2:]), 1, 0)

    qpos = jnp.arange(n_blk * Q_BLOCK).reshape(n_blk, Q_BLOCK)
    out = lax.map(lambda args: fn(args[0], *args[1:]), (qpos,) + tuple(blk(a) for a in qs))
    out = jnp.moveaxis(out, 0, 1)
    return out.reshape(out.shape[0], n_blk * Q_BLOCK, *out.shape[3:])[:, :T]


def dsa_prompt(bq, bk, bv, iq, ik, iw, top_k):
    T = bq.shape[1]
    kpos = jnp.arange(T)

    def fn(qp, q_b, iq_b, iw_b):
        allowed = kpos[None, :] <= qp[:, None]
        _, idx = lax.top_k(indexer_scores(iq_b, iw_b, ik, allowed), top_k)
        valid = idx <= qp[None, :, None]
        return sparse_attend(q_b, gather_rows(bk, idx), gather_rows(bv, idx), valid)

    return query_blocks(fn, T, bq, iq, iw)


def diff_prompt(cq, ck, cv, lam):
    T = cq.shape[1]
    kpos = jnp.arange(T)

    def fn(qp, q_b):
        return diff_attend(q_b, ck, cv, kpos[None, :] <= qp[:, None], lam)

    return query_blocks(fn, T, cq)


def gather_pages(cache_l, page_table):
    g = cache_l[page_table]
    return g.reshape(g.shape[0], g.shape[1] * g.shape[2], *g.shape[3:])


def dsa_sample(bq, bk, bv, iq, ik, iw, ck_l, cv_l, ckidx_l, page_table, top_k):
    N, T = bq.shape[:2]
    past = page_table.shape[1] * PAGE_SIZE
    kidx_all = jnp.concatenate([gather_pages(ckidx_l, page_table).astype(ik.dtype), ik], axis=1)
    qp = past + jnp.arange(T)
    allowed = jnp.arange(past + T)[None, :] <= qp[:, None]
    _, idx = lax.top_k(indexer_scores(iq, iw, kidx_all, allowed), top_k)
    pidx = jnp.minimum(idx, past - 1)
    phys = jnp.take_along_axis(page_table, (pidx // PAGE_SIZE).reshape(N, -1), axis=1).reshape(idx.shape)
    off = pidx % PAGE_SIZE
    is_new = (idx >= past)[..., None, None]
    nidx = jnp.clip(idx - past, 0, T - 1)
    k_sel = jnp.where(is_new, gather_rows(bk, nidx), ck_l[phys, off].astype(bk.dtype))
    v_sel = jnp.where(is_new, gather_rows(bv, nidx), cv_l[phys, off].astype(bv.dtype))
    return sparse_attend(bq, k_sel, v_sel, idx <= qp[None, :, None])


def diff_sample(cq, ck, cv, ck_l, cv_l, page_table, lam):
    N, T = cq.shape[:2]
    past = page_table.shape[1] * PAGE_SIZE
    k_past = gather_pages(ck_l, page_table).reshape(N, past, H_DIFF, 2, D_DIFF).astype(ck.dtype)
    v_past = gather_pages(cv_l, page_table).astype(cv.dtype)
    k_all = jnp.concatenate([k_past, ck], axis=1)
    v_all = jnp.concatenate([v_past, cv], axis=1)
    allowed = jnp.arange(past + T)[None, :] <= (past + jnp.arange(T))[:, None]
    return diff_attend(cq, k_all, v_all, allowed, lam)


def diff_lambda(lq1, lk1, lq2, lk2, lam_init):
    return (jnp.exp(jnp.sum(lq1.astype(F32) * lk1.astype(F32)))
            - jnp.exp(jnp.sum(lq2.astype(F32) * lk2.astype(F32))) + lam_init)


def mixer_inputs(h, pos, w_in_l, dsa_qn, dsa_kn, diff_qn, diff_kn):
    N, T = h.shape[:2]
    (rq, rk, rv, rg, bq, bk, bv, iq, ik, iw, cq, ck, cv) = jnp.split(
        h @ w_in_l, np.cumsum(PROJ_SIZES)[:-1].tolist(), axis=-1)
    rq = rope(rq.reshape(N, T, H_RET, D_RET), pos, D_RET, RET_THETA)
    rk = rope(rk.reshape(N, T, H_RET, D_RET), pos, D_RET, RET_THETA) * (D_RET ** -0.5)
    rv = rv.reshape(N, T, H_RET, D_RET)
    bq = rope(rmsnorm(bq.reshape(N, T, H_DSA, D_DSA), dsa_qn), pos, D_DSA // ROT_FRAC, ROPE_THETA)
    bk = rope(rmsnorm(bk.reshape(N, T, H_DSA, D_DSA), dsa_kn), pos, D_DSA // ROT_FRAC, ROPE_THETA)
    bv = bv.reshape(N, T, H_DSA, D_DSA)
    iq = rope(iq.reshape(N, T, H_IDX, D_IDX), pos, D_IDX // ROT_FRAC, ROPE_THETA)
    ik = rope(ik.reshape(N, T, 1, D_IDX), pos, D_IDX // ROT_FRAC, ROPE_THETA)[:, :, 0]
    iw = iw * (H_IDX * D_IDX) ** -0.5
    cq = rope(rmsnorm(cq.reshape(N, T, 2 * H_DIFF, D_DIFF), diff_qn), pos, D_DIFF // ROT_FRAC,
              ROPE_THETA).reshape(N, T, H_DIFF, 2, D_DIFF)
    ck = rope(rmsnorm(ck.reshape(N, T, 2 * H_DIFF, D_DIFF), diff_kn), pos, D_DIFF // ROT_FRAC,
              ROPE_THETA).reshape(N, T, H_DIFF, 2, D_DIFF)
    cv = cv.reshape(N, T, H_DIFF, V_DIFF)
    return (rq, rk, rv, rg, bq, bk, bv, iq, ik, iw, cq, ck, cv)


def conv_ffn(h, conv_state, wg, wu, cw, cb, wd):
    T = h.shape[1]
    g = h @ wg
    u = h @ wu
    gp = jnp.concatenate([conv_state.astype(g.dtype), g], axis=1)
    gc = sum(cw[j] * gp[:, j:j + T] for j in range(CONV_W)) + cb
    return (jax.nn.silu(gc) * u) @ wd, gp[:, T:]


def layer_step(x, pos, conv_state, mixers, lw):
    (g_mix, w_in_l, ret_gn_l, dsa_qn, dsa_kn, diff_qn, diff_kn, subln_l, lam_init, w_out_l,
     g_ffn, wg, wu, cw, cb, wd) = lw
    N, T = x.shape[:2]
    p = mixer_inputs(rmsnorm(x, g_mix), pos, w_in_l, dsa_qn, dsa_kn, diff_qn, diff_kn)
    o_ret, ret_state, o_dsa, o_diff = mixers(p)
    a = jax.nn.silu(p[3].astype(F32)) * rmsnorm(o_ret, ret_gn_l.reshape(H_RET, D_RET)).reshape(N, T, -1)
    c = rmsnorm(o_diff, subln_l).astype(F32) * (1.0 - lam_init)
    cat = jnp.concatenate([a.astype(x.dtype), o_dsa.reshape(N, T, -1).astype(x.dtype),
                           c.reshape(N, T, -1).astype(x.dtype)], axis=-1)
    x = x + cat @ w_out_l
    f, conv_new = conv_ffn(rmsnorm(x, g_ffn), conv_state, wg, wu, cw, cb, wd)
    rows = (p[5], p[6], p[8], p[11].reshape(N, T, H_DIFF, 2 * D_DIFF), p[12])
    return x + f, ret_state, conv_new, rows


def setup_inputs(seed: int = 0) -> dict:
    key = jax.random.key(seed)
    ks = iter(jax.random.split(key, 40))

    def nrm(shape, scale):
        return jax.random.normal(next(ks), shape, F32) * scale

    def gain(shape):
        return 1.0 + nrm(shape, 0.02)

    n_pages = PAST_LEN // PAGE_SIZE
    n_used = DEC_BATCH * n_pages
    n_pool = n_used + n_used // 4
    page_table = jax.random.permutation(next(ks), n_pool)[:n_used].reshape(DEC_BATCH, n_pages).astype(jnp.int32)
    return {
        'x_prompt': nrm((BATCH, SEQ, D_MODEL), 1.0),
        'x_sample': nrm((DEC_BATCH, DEC_SEQ, D_MODEL), 1.0),
        'cache_dsa_k': nrm((DEPTH, n_pool, PAGE_SIZE, H_DSA, D_DSA), 1.0),
        'cache_dsa_v': nrm((DEPTH, n_pool, PAGE_SIZE, H_DSA, D_DSA), 1.0),
        'cache_dsa_kidx': nrm((DEPTH, n_pool, PAGE_SIZE, D_IDX), 1.0),
        'cache_diff_k': nrm((DEPTH, n_pool, PAGE_SIZE, H_DIFF, 2 * D_DIFF), 1.0),
        'cache_diff_v': nrm((DEPTH, n_pool, PAGE_SIZE, H_DIFF, V_DIFF), 1.0),
        'state_ret': nrm((DEPTH, DEC_BATCH, H_RET, D_RET, D_RET), 0.5),
        'state_conv': nrm((DEPTH, DEC_BATCH, CONV_W - 1, D_FF), 1.0),
        'page_table': page_table,
        'meta': nrm((N_META, D_MODEL), 1.0),
        'norm_mix_g': gain((DEPTH, D_MODEL)),
        'w_in': nrm((DEPTH, D_MODEL, IN_WIDTH), D_MODEL ** -0.5),
        'ret_gn_g': gain((DEPTH, H_RET * D_RET)),
        'dsa_qn_g': gain((DEPTH, D_DSA)),
        'dsa_kn_g': gain((DEPTH, D_DSA)),
        'diff_qn_g': gain((DEPTH, D_DIFF)),
        'diff_kn_g': gain((DEPTH, D_DIFF)),
        'diff_lq1': nrm((DEPTH, D_DIFF), 0.1),
        'diff_lk1': nrm((DEPTH, D_DIFF), 0.1),
        'diff_lq2': nrm((DEPTH, D_DIFF), 0.1),
        'diff_lk2': nrm((DEPTH, D_DIFF), 0.1),
        'diff_subln_g': gain((DEPTH, V_DIFF)),
        'w_out': nrm((DEPTH, MIX, D_MODEL), MIX ** -0.5),
        'norm_ffn_g': gain((DEPTH, D_MODEL)),
        'ffn_w_gate': nrm((DEPTH, D_MODEL, D_FF), D_MODEL ** -0.5),
        'ffn_w_up': nrm((DEPTH, D_MODEL, D_FF), D_MODEL ** -0.5),
        'ffn_conv_w': nrm((DEPTH, CONV_W, D_FF), CONV_W ** -0.5),
        'ffn_conv_b': nrm((DEPTH, D_FF), 0.02),
        'ffn_w_down': nrm((DEPTH, D_FF, D_MODEL), D_FF ** -0.5),
    }


def reference(x_prompt, x_sample, cache_dsa_k, cache_dsa_v, cache_dsa_kidx, cache_diff_k, cache_diff_v,
              state_ret, state_conv, page_table, meta, norm_mix_g, w_in, ret_gn_g, dsa_qn_g, dsa_kn_g,
              diff_qn_g, diff_kn_g, diff_lq1, diff_lk1, diff_lq2, diff_lk2, diff_subln_g, w_out,
              norm_ffn_g, ffn_w_gate, ffn_w_up, ffn_conv_w, ffn_conv_b, ffn_w_down):
    B = x_prompt.shape[0]
    TS = x_sample.shape[1]
    xp = jnp.concatenate([jnp.broadcast_to(meta.astype(x_prompt.dtype), (B, N_META, D_MODEL)), x_prompt], axis=1)
    TP = xp.shape[1]
    past = page_table.shape[1] * PAGE_SIZE
    pos_p = jnp.arange(TP, dtype=F32)
    pos_s = jnp.arange(TS, dtype=F32) + float(past)
    k_p = min(TOPK_MAX, TP // 4)
    k_s = min(TOPK_MAX, (past + TS) // 4)
    lg = retention_decay()
    xs = x_sample
    conv0 = jnp.zeros((B, CONV_W - 1, D_FF), xp.dtype)
    outs_p = [[] for _ in range(7)]
    outs_s = [[] for _ in range(7)]
    for l in range(DEPTH):
        lam_init = 0.8 - 0.6 * math.exp(-0.3 * l)
        lam = diff_lambda(diff_lq1[l], diff_lk1[l], diff_lq2[l], diff_lk2[l], lam_init)
        lw = (norm_mix_g[l], w_in[l], ret_gn_g[l], dsa_qn_g[l], dsa_kn_g[l], diff_qn_g[l], diff_kn_g[l],
              diff_subln_g[l], lam_init, w_out[l], norm_ffn_g[l], ffn_w_gate[l], ffn_w_up[l],
              ffn_conv_w[l], ffn_conv_b[l], ffn_w_down[l])

        def prompt_mixers(p):
            o_ret, S = retention_prompt(p[0], p[1], p[2], lg)
            return o_ret, S, dsa_prompt(*p[4:10], k_p), diff_prompt(*p[10:13], lam)

        def sample_mixers(p):
            o_ret, S = retention_chunk(state_ret[l], p[0], p[1], p[2], lg)
            o_dsa = dsa_sample(*p[4:10], cache_dsa_k[l], cache_dsa_v[l], cache_dsa_kidx[l], page_table, k_s)
            o_diff = diff_sample(*p[10:13], cache_diff_k[l], cache_diff_v[l], page_table, lam)
            return o_ret, S, o_dsa, o_diff

        xp, S_p, c_p, rows_p = layer_step(xp, pos_p, conv0, prompt_mixers, lw)
        xs, S_s, c_s, rows_s = layer_step(xs, pos_s, state_conv[l], sample_mixers, lw)
        for lst, a in zip(outs_p, (S_p, c_p) + rows_p):
            lst.append(a)
        for lst, a in zip(outs_s, (S_s, c_s) + rows_s):
            lst.append(a)
    ret_p, conv_p, dsak_p, dsav_p, kidx_p, diffk_p, diffv_p = (jnp.stack(a) for a in outs_p)
    ret_s, conv_s, dsak_s, dsav_s, kidx_s, diffk_s, diffv_s = (jnp.stack(a) for a in outs_s)
    y_prompt = xp[:, N_META:]
    y_sample = xs
    return (y_prompt, y_sample, ret_p, ret_s, conv_p, conv_s, dsak_p, dsak_s, dsav_p, dsav_s,
            kidx_p, kidx_s, diffk_p, diffk_s, diffv_p, diffv_s)
```

```python
import functools
import math
import struct

import jax
import jax.numpy as jnp
from jax import lax
from jax.experimental import pallas as pl
from jax.experimental.pallas import tpu as pltpu

F32 = jnp.float32
BF16 = jnp.bfloat16
I32 = jnp.int32

N_META = 16
H_RET, D_RET = 6, 64
H_DSA, D_DSA = 5, 64
H_IDX, D_IDX = 8, 64
TOPK_MAX = 256
H_DIFF, D_DIFF, V_DIFF = 5, 32, 64
CONV_W = 3
ROPE_THETA = 500000.0
RET_THETA = 10000.0
ROT_FRAC = 4
PAGE_SIZE = 128
NEG = -1e30
EPS = 1e-6
PROJ_SIZES = ((H_RET * D_RET,) * 4 + (H_DSA * D_DSA,) * 3 + (H_IDX * D_IDX, D_IDX, H_IDX)
              + (2 * H_DIFF * D_DIFF, 2 * H_DIFF * D_DIFF, H_DIFF * V_DIFF))

LANE = 128
VMEM_LIMIT = 56 * 1024 * 1024

W3 = 3 * LANE
GROUP_WIDTHS = (W3, W3, W3, W3, W3, W3, W3, H_IDX * D_IDX, LANE, LANE, W3, W3, W3)
GROUP_OFFS = tuple(int(sum(GROUP_WIDTHS[:i])) for i in range(len(GROUP_WIDTHS)))
IN_PAD = int(sum(GROUP_WIDTHS))
(O_RQ, O_RK, O_RV, O_RG, O_BQ, O_BK, O_BV, O_IQ, O_IK, O_IW, O_CQ, O_CK, O_CV) = GROUP_OFFS
MIXW = H_DSA * D_DSA

ROW_TILE = 256
Q_TILE = 128
K_CHUNK = 256
RET_CHUNK = 128
SAMPLE_TILE = 16


def _float_key(v):
    b = struct.unpack("<i", struct.pack("<f", v))[0]
    return b if b >= 0 else b ^ 0x7FFFFFFF


NEG_KEY = _float_key(NEG)
INT_MAX = 2 ** 31 - 1
INT_MIN = -2 ** 31


def _cparams(sem):
    return pltpu.CompilerParams(dimension_semantics=sem, vmem_limit_bytes=VMEM_LIMIT)


def _dot(a, b):
    return jnp.dot(a, b, preferred_element_type=F32)


def _dot_nt(a, b):
    return lax.dot_general(a, b, (((1,), (1,)), ((), ())), preferred_element_type=F32)


def _silu(x):
    return x * (1.0 / (1.0 + jnp.exp(-x)))


def _seg_rsqrt(x, seg_ref, width):
    x2 = x * x
    hi = x2.astype(BF16)
    lo = (x2 - hi.astype(F32)).astype(BF16)
    seg = seg_ref[...]
    ss = _dot(hi, seg) + _dot(lo, seg)
    return lax.rsqrt(ss * (1.0 / width) + EPS)


def _rope_chunk(x, c, s1, s2, half):
    return x * c + pltpu.roll(x, LANE - half, 1) * s1 + pltpu.roll(x, half, 1) * s2


def _lane_mask(shape, lo, hi):
    lane = lax.broadcasted_iota(I32, shape, len(shape) - 1)
    return (lane >= lo) & (lane < hi)


def _inproj_kernel(x_ref, g_ref, w_ref, rc_ref, rs1_ref, rs2_ref, dc_ref, ds1_ref, ds2_ref,
                   fc_ref, fs1_ref, fs2_ref, qn_ref, kn_ref, fqn_ref, fkn_ref, b64_ref, b32_ref,
                   ret4_ref, bq16_ref, bk16_ref, bv16_ref, iq16_ref, ik16_ref, iw_ref,
                   cq16_ref, ck16_ref, cv16_ref, bk32_ref, bv32_ref, ik32_ref, ck32_ref, cv32_ref):
    x = x_ref[...]
    h = x * lax.rsqrt(jnp.mean(x * x, axis=-1, keepdims=True) + EPS) * g_ref[...]
    hb = h.astype(BF16)

    def proj(off, j):
        return _dot(hb, w_ref[:, off + j * LANE: off + (j + 1) * LANE])

    ret_tab = (rc_ref[...], rs1_ref[...], rs2_ref[...], D_RET // 2)
    dsa_tab = (dc_ref[...], ds1_ref[...], ds2_ref[...], D_DSA // ROT_FRAC // 2)
    dif_tab = (fc_ref[...], fs1_ref[...], fs2_ref[...], D_DIFF // ROT_FRAC // 2)

    def rope(p, tab):
        return _rope_chunk(p, *tab)

    for j in range(3):
        sl = slice(j * LANE, (j + 1) * LANE)
        ret4_ref[:, O_RQ + j * LANE: O_RQ + (j + 1) * LANE] = rope(proj(O_RQ, j), ret_tab)
        ret4_ref[:, O_RK + j * LANE: O_RK + (j + 1) * LANE] = rope(proj(O_RK, j), ret_tab) * (D_RET ** -0.5)
        ret4_ref[:, O_RV + j * LANE: O_RV + (j + 1) * LANE] = proj(O_RV, j)
        ret4_ref[:, O_RG + j * LANE: O_RG + (j + 1) * LANE] = proj(O_RG, j)
        p = proj(O_BQ, j)
        bq = rope(p * _seg_rsqrt(p, b64_ref, D_DSA) * qn_ref[...], dsa_tab) * (D_DSA ** -0.5)
        bq16_ref[:, sl] = bq.astype(BF16)
        p = proj(O_BK, j)
        bk = rope(p * _seg_rsqrt(p, b64_ref, D_DSA) * kn_ref[...], dsa_tab)
        bk16_ref[:, sl] = bk.astype(BF16)
        bv = proj(O_BV, j)
        bv16_ref[:, sl] = bv.astype(BF16)
        p = proj(O_CQ, j)
        cq = rope(p * _seg_rsqrt(p, b32_ref, D_DIFF) * fqn_ref[...], dif_tab) * (D_DIFF ** -0.5)
        cq16_ref[:, sl] = cq.astype(BF16)
        p = proj(O_CK, j)
        ck = rope(p * _seg_rsqrt(p, b32_ref, D_DIFF) * fkn_ref[...], dif_tab)
        ck16_ref[:, sl] = ck.astype(BF16)
        cv = proj(O_CV, j)
        cv16_ref[:, sl] = cv.astype(BF16)
        w = LANE if j < 2 else MIXW - 2 * LANE
        bk32_ref[:, j * LANE: j * LANE + w] = bk[:, :w]
        bv32_ref[:, j * LANE: j * LANE + w] = bv[:, :w]
        ck32_ref[:, j * LANE: j * LANE + w] = ck[:, :w]
        cv32_ref[:, j * LANE: j * LANE + w] = cv[:, :w]
    for j in range(H_IDX * D_IDX // LANE):
        iq16_ref[:, j * LANE:(j + 1) * LANE] = rope(proj(O_IQ, j), dsa_tab).astype(BF16)
    ik = rope(proj(O_IK, 0), dsa_tab)
    ik32_ref[...] = ik[:, :D_IDX]
    ik16_ref[...] = jnp.where(_lane_mask(ik.shape, 0, D_IDX), ik, pltpu.roll(ik, D_IDX, 1)).astype(BF16)
    iw_ref[...] = proj(O_IW, 0) * ((H_IDX * D_IDX) ** -0.5)


def _inproj(x2d, gmix, w, tabs, gains, segs, tm):
    rows, dm = x2d.shape
    tab_rows = tabs[0].shape[0]
    nb_tab = tab_rows // tm
    grid = (rows // tm,)
    row = lambda i: (i, 0)
    const = lambda i: (0, 0)
    tab = lambda i: (i % nb_tab, 0)
    in_specs = ([pl.BlockSpec((tm, dm), row), pl.BlockSpec((1, dm), const), pl.BlockSpec((dm, IN_PAD), const)]
                + [pl.BlockSpec((tm, LANE), tab)] * 9
                + [pl.BlockSpec((1, LANE), const)] * 4
                + [pl.BlockSpec((LANE, LANE), const)] * 2)
    widths = [(4 * W3, F32), (W3, BF16), (W3, BF16), (W3, BF16), (H_IDX * D_IDX, BF16), (LANE, BF16), (LANE, F32),
              (W3, BF16), (W3, BF16), (W3, BF16), (MIXW, F32), (MIXW, F32), (D_IDX, F32), (MIXW, F32), (MIXW, F32)]
    out_shape = [jax.ShapeDtypeStruct((rows, wd), dt) for wd, dt in widths]
    out_specs = [pl.BlockSpec((tm, wd), row) for wd, _ in widths]
    return pl.pallas_call(
        _inproj_kernel, grid=grid, in_specs=in_specs, out_specs=out_specs, out_shape=out_shape,
        compiler_params=_cparams(("parallel",)),
    )(x2d, gmix, w, *tabs, *gains, *segs)


def _ret_lg(h):
    return math.log(1.0 - 2.0 ** (-5.0 - h))


def _gate(o, g, gn, b64_ref):
    return _silu(g) * (o * _seg_rsqrt(o, b64_ref, D_RET) * gn)


def _ret_prompt_kernel(ret4_ref, gn_ref, b64_ref, a_ref, sfin_ref, s_ref, *, tp):
    c = pl.program_id(1)
    ch = RET_CHUNK

    @pl.when(c == 0)
    def _():
        s_ref[...] = jnp.zeros_like(s_ref)

    valid = jnp.clip(tp - c * ch, 0, ch).astype(F32)
    ri = lax.broadcasted_iota(I32, (ch, 1), 0).astype(F32)
    rel = (lax.broadcasted_iota(I32, (ch, ch), 0) - lax.broadcasted_iota(I32, (ch, ch), 1)).astype(F32)
    lane = lax.broadcasted_iota(I32, (1, LANE), 1)
    rowi = lax.broadcasted_iota(I32, (LANE, 1), 0)
    for p in range(H_RET // 2):
        sl = slice(p * LANE, (p + 1) * LANE)
        lg_a, lg_b = _ret_lg(2 * p), _ret_lg(2 * p + 1)
        q = ret4_ref[:, O_RQ + p * LANE: O_RQ + (p + 1) * LANE]
        k = ret4_ref[:, O_RK + p * LANE: O_RK + (p + 1) * LANE]
        v = ret4_ref[:, O_RV + p * LANE: O_RV + (p + 1) * LANE]
        k16 = k.astype(BF16)
        v16 = v.astype(BF16)
        s_pair = s_ref[p]
        s16 = s_pair.astype(BF16)
        outs = []
        for half, lg in ((0, lg_a), (1, lg_b)):
            qm = jnp.where(_lane_mask(q.shape, half * D_RET, (half + 1) * D_RET), q, 0.0).astype(BF16)
            dmat = jnp.where(rel >= 0.0, jnp.exp(jnp.maximum(rel, 0.0) * lg), 0.0)
            inner = _dot_nt(qm, k16) * dmat
            outs.append(_dot(inner.astype(BF16), v16) + _dot(qm, s16) * jnp.exp((ri + 1.0) * lg))
        o_pair = jnp.where(lane < D_RET, outs[0], outs[1])
        lgv = jnp.where(lane < D_RET, lg_a, lg_b)
        kw = k * jnp.where(ri < valid, jnp.exp((valid - 1.0 - ri) * lgv), 0.0)
        lgc = jnp.where(rowi < D_RET, lg_a, lg_b)
        s_new = jnp.exp(valid * lgc) * s_pair + _dot(kw.T.astype(BF16), v16)
        s_ref[p] = s_new
        g = ret4_ref[:, O_RG + p * LANE: O_RG + (p + 1) * LANE]
        a_ref[:, sl] = _gate(o_pair, g, gn_ref[:, sl], b64_ref)

    @pl.when(c == pl.num_programs(1) - 1)
    def _():
        for h in range(H_RET):
            blk = s_ref[h // 2]
            o = (h % 2) * D_RET
            sfin_ref[0, h] = blk[o:o + D_RET, o:o + D_RET]


def _ret_prompt(ret4, gn, b64, n, tpad, tp):
    nc = tpad // RET_CHUNK
    return pl.pallas_call(
        functools.partial(_ret_prompt_kernel, tp=tp),
        grid=(n, nc),
        in_specs=[pl.BlockSpec((RET_CHUNK, 4 * W3), lambda b, c: (b * nc + c, 0)),
                  pl.BlockSpec((1, W3), lambda b, c: (0, 0)),
                  pl.BlockSpec((LANE, LANE), lambda b, c: (0, 0))],
        out_specs=[pl.BlockSpec((RET_CHUNK, W3), lambda b, c: (b * nc + c, 0)),
                   pl.BlockSpec((1, H_RET, D_RET, D_RET), lambda b, c: (b, 0, 0, 0))],
        out_shape=[jax.ShapeDtypeStruct((n * tpad, W3), F32),
                   jax.ShapeDtypeStruct((n, H_RET, D_RET, D_RET), F32)],
        scratch_shapes=[pltpu.VMEM((H_RET // 2, LANE, LANE), F32)],
        compiler_params=_cparams(("parallel", "arbitrary")),
    )(ret4, gn, b64)


def _ret_sample_kernel(qt_ref, kt_ref, v_ref, g_ref, gn_ref, st_ref, a_ref, snew_ref, o_ref):
    nb = v_ref.shape[0]
    for j in range(nb):
        for h in range(H_RET):
            gamma = math.exp(_ret_lg(h))
            qcol = qt_ref[0, h * D_RET:(h + 1) * D_RET, j:j + 1]
            kcol = kt_ref[0, h * D_RET:(h + 1) * D_RET, j:j + 1]
            vrow = v_ref[j, h:h + 1, :]
            s_new = gamma * st_ref[j, h] + kcol * vrow
            snew_ref[j, h] = s_new
            o_ref[j, h:h + 1, :] = jnp.sum(qcol * s_new, axis=0, keepdims=True)
    o = o_ref[...]
    r = lax.rsqrt(jnp.mean(o * o, axis=-1, keepdims=True) + EPS)
    a_ref[...] = _silu(g_ref[...]) * (o * r * gn_ref[...])


def _ret_sample(qt, kt, v3, g3, gn2, state4, layer, ns):
    nb = qt.shape[2]
    nblk = ns // nb
    blk3 = pl.BlockSpec((nb, H_RET, D_RET), lambda i: (i, 0, 0))
    tsp = pl.BlockSpec((1, H_RET * D_RET, nb), lambda i: (i, 0, 0))
    return pl.pallas_call(
        _ret_sample_kernel,
        grid=(nblk,),
        in_specs=[tsp, tsp, blk3, blk3,
                  pl.BlockSpec((H_RET, D_RET), lambda i: (0, 0)),
                  pl.BlockSpec((nb, H_RET, D_RET, D_RET), lambda i: (layer * nblk + i, 0, 0, 0))],
        out_specs=[blk3, pl.BlockSpec((nb, H_RET, D_RET, D_RET), lambda i: (i, 0, 0, 0))],
        out_shape=[jax.ShapeDtypeStruct((ns, H_RET, D_RET), F32),
                   jax.ShapeDtypeStruct((ns, H_RET, D_RET, D_RET), F32)],
        scratch_shapes=[pltpu.VMEM((nb, H_RET, D_RET), F32)],
        compiler_params=_cparams(("parallel",)),
    )(qt, kt, v3, g3, gn2, state4)


def _to_key(s):
    b = lax.bitcast_convert_type(s, I32)
    return jnp.where(b >= 0, b, b ^ jnp.int32(0x7FFFFFFF))


def _fold_lanes(x):
    acc = x[:, :LANE]
    for t in range(1, x.shape[1] // LANE):
        acc = acc + x[:, t * LANE:(t + 1) * LANE]
    return acc


def _select_bias(keys_ref, bias_ref, nkc, k, pos_bits):
    _, rows, kc = keys_ref.shape

    def count(ones):
        def body(c, acc):
            kpos = c * kc + lax.broadcasted_iota(I32, (rows, kc), 1)
            return acc + _fold_lanes(ones(keys_ref[c], kpos))
        acc = lax.fori_loop(0, nkc, body, jnp.zeros((rows, LANE), F32))
        return jnp.sum(acc, axis=-1, keepdims=True)

    def thr_step(it, thr):
        cand = thr + lax.shift_left(jnp.int32(1), 31 - it)
        cand_b = jnp.broadcast_to(cand, (rows, kc))
        tot = count(lambda kk, kpos: jnp.where(kk >= cand_b, 1.0, 0.0))
        return jnp.where(tot >= k, cand, thr)

    thr = lax.fori_loop(0, 32, thr_step, jnp.full((rows, 1), INT_MIN, I32))
    thr_b = jnp.broadcast_to(thr, (rows, kc))
    c_gt = count(lambda kk, kpos: jnp.where(kk > thr_b, 1.0, 0.0))
    c_ge = count(lambda kk, kpos: jnp.where(kk >= thr_b, 1.0, 0.0))
    need = k - c_gt
    excess = jnp.where(thr > NEG_KEY, c_ge - k, 0.0)

    def tie_search():
        def step(it, lo):
            cand = lo + lax.shift_left(jnp.int32(1), pos_bits - 1 - it)
            cand_b = jnp.broadcast_to(cand, (rows, kc))
            f = count(lambda kk, kpos: jnp.where(kk == thr_b, jnp.where(kpos < cand_b, 1.0, 0.0), 0.0))
            return jnp.where(f < need, cand, lo)
        return lax.fori_loop(0, pos_bits, step, jnp.zeros((rows, 1), I32))

    last = lax.cond(jnp.max(excess) > 0.0, tie_search, lambda: jnp.full((rows, 1), INT_MAX, I32))
    last_b = jnp.broadcast_to(last, (rows, kc))

    def emit(c, carry):
        kk = keys_ref[c]
        kpos = c * kc + lax.broadcasted_iota(I32, (rows, kc), 1)
        tie = jnp.where(kpos <= last_b, 0.0, NEG)
        sel = jnp.where(kk > thr_b, 0.0, jnp.where(kk == thr_b, tie, NEG))
        bias_ref[c] = jnp.where(kk > NEG_KEY, sel, NEG)
        return carry

    lax.fori_loop(0, nkc, emit, 0)


def _online_update(carry, s, v16):
    m, l, acc = carry
    m_new = jnp.maximum(m, jnp.max(s, axis=-1, keepdims=True))
    alpha = jnp.exp(m - m_new)
    p = jnp.exp(s - m_new)
    l = alpha * l + jnp.sum(p, axis=-1, keepdims=True)
    acc = alpha * acc + _dot(p.astype(BF16), v16)
    return m_new, l, acc


def _online_init(rows, width):
    return (jnp.full((rows, 1), -jnp.inf, F32), jnp.zeros((rows, 1), F32), jnp.zeros((rows, width), F32))


def _dsa_prompt_kernel(bq_ref, iq_ref, iw_ref, ik_ref, bk_ref, bv_ref, o_ref, keys_ref, bias_ref, *, top_k, pos_bits):
    b = pl.program_id(1)
    tq, kc = Q_TILE, K_CHUNK
    q0 = b * tq
    nkc = (q0 + tq + kc - 1) // kc
    qpos = q0 + lax.broadcasted_iota(I32, (tq, 1), 0)
    iw = iw_ref[...]

    iq_heads = []
    for h in range(H_IDX):
        ch = iq_ref[:, (h // 2) * LANE:(h // 2 + 1) * LANE]
        iq_heads.append(jnp.where(_lane_mask(ch.shape, (h % 2) * D_IDX, (h % 2 + 1) * D_IDX), ch, 0).astype(BF16))

    def score_body(c, carry):
        kch = ik_ref[pl.ds(pl.multiple_of(c * kc, kc), kc), :]
        acc = jnp.zeros((tq, kc), F32)
        for h in range(H_IDX):
            acc = acc + iw[:, h:h + 1] * jnp.maximum(_dot_nt(iq_heads[h], kch), 0.0)
        acc = jnp.where(acc == 0.0, 0.0, acc)
        kpos = c * kc + lax.broadcasted_iota(I32, (1, kc), 1)
        keys_ref[c] = _to_key(jnp.where(kpos <= qpos, acc, NEG))
        return carry

    lax.fori_loop(0, nkc, score_body, 0)
    _select_bias(keys_ref, bias_ref, nkc, float(top_k), pos_bits)

    lane = lax.broadcasted_iota(I32, (1, LANE), 1)
    outs = []
    for h in range(H_DSA):
        pr, half = h // 2, h % 2
        ch = bq_ref[:, pr * LANE:(pr + 1) * LANE]
        qm = jnp.where(_lane_mask(ch.shape, half * D_DSA, (half + 1) * D_DSA), ch, 0).astype(BF16)

        def body(c, carry, pr=pr, qm=qm):
            rows = pl.ds(pl.multiple_of(c * kc, kc), kc)
            s = _dot_nt(qm, bk_ref[rows, pr * LANE:(pr + 1) * LANE]) + bias_ref[c]
            return _online_update(carry, s, bv_ref[rows, pr * LANE:(pr + 1) * LANE])

        m, l, acc = lax.fori_loop(0, nkc, body, _online_init(tq, LANE))
        outs.append(acc * (1.0 / l))
    outs.append(jnp.zeros_like(outs[0]))
    for pr in range(3):
        o_ref[:, pr * LANE:(pr + 1) * LANE] = jnp.where(lane < D_DSA, outs[2 * pr], outs[2 * pr + 1])


def _dsa_prompt(bq16, iq16, iw, ik16, bk16, bv16, n, tpad, top_k):
    nq = tpad // Q_TILE
    nkc = tpad // K_CHUNK
    pos_bits = max(1, int(tpad).bit_length())
    qrow = lambda b, i: (b * nq + i, 0)
    seq = lambda b, i: (b, 0)
    return pl.pallas_call(
        functools.partial(_dsa_prompt_kernel, top_k=top_k, pos_bits=pos_bits),
        grid=(n, nq),
        in_specs=[pl.BlockSpec((Q_TILE, W3), qrow), pl.BlockSpec((Q_TILE, H_IDX * D_IDX), qrow),
                  pl.BlockSpec((Q_TILE, LANE), qrow),
                  pl.BlockSpec((tpad, LANE), seq), pl.BlockSpec((tpad, W3), seq), pl.BlockSpec((tpad, W3), seq)],
        out_specs=pl.BlockSpec((Q_TILE, W3), qrow),
        out_shape=jax.ShapeDtypeStruct((n * tpad, W3), F32),
        scratch_shapes=[pltpu.VMEM((nkc, Q_TILE, K_CHUNK), I32), pltpu.VMEM((nkc, Q_TILE, K_CHUNK), F32)],
        compiler_params=_cparams(("parallel", "arbitrary")),
    )(bq16, iq16, iw, ik16, bk16, bv16)


def _diff_finish(o1, o2, l1, l2, lam, post, sub, b64_ref):
    o = o1 * (1.0 / l1) - lam * (o2 * (1.0 / l2))
    return o * _seg_rsqrt(o, b64_ref, V_DIFF) * sub * post


def _diff_prompt_kernel(sc_ref, cq_ref, ck_ref, cv_ref, sub_ref, b64_ref, o_ref):
    b = pl.program_id(1)
    tq, kc = Q_TILE, K_CHUNK
    q0 = b * tq
    nfull = q0 // kc
    qpos = q0 + lax.broadcasted_iota(I32, (tq, 1), 0)
    lam, post = sc_ref[0], sc_ref[1]
    lane = lax.broadcasted_iota(I32, (1, LANE), 1)
    res = []
    for h in range(H_DIFF):
        pr, half = h // 2, h % 2
        ch = cq_ref[:, pr * LANE:(pr + 1) * LANE]
        base = half * V_DIFF
        q1 = jnp.where(_lane_mask(ch.shape, base, base + D_DIFF), ch, 0).astype(BF16)
        q2 = jnp.where(_lane_mask(ch.shape, base + D_DIFF, base + 2 * D_DIFF), ch, 0).astype(BF16)

        def step(c, carry, masked, pr=pr, q1=q1, q2=q2):
            rows = pl.ds(pl.multiple_of(c * kc, kc), kc)
            k16 = ck_ref[rows, pr * LANE:(pr + 1) * LANE]
            v16 = cv_ref[rows, pr * LANE:(pr + 1) * LANE]
            s1 = _dot_nt(q1, k16)
            s2 = _dot_nt(q2, k16)
            if masked:
                ok = (c * kc + lax.broadcasted_iota(I32, (1, kc), 1)) <= qpos
                s1 = jnp.where(ok, s1, NEG)
                s2 = jnp.where(ok, s2, NEG)
            return (_online_update(carry[0], s1, v16), _online_update(carry[1], s2, v16))

        carry = (_online_init(tq, LANE), _online_init(tq, LANE))
        carry = lax.fori_loop(0, nfull, functools.partial(step, masked=False), carry)
        (m1, l1, a1), (m2, l2, a2) = step(nfull, carry, True)
        res.append((a1, a2, l1, l2))
    res.append(res[-1])
    for pr in range(3):
        ra, rb = res[2 * pr], res[2 * pr + 1]
        pick = lambda t: jnp.where(lane < V_DIFF, ra[t], rb[t])
        o_ref[:, pr * LANE:(pr + 1) * LANE] = _diff_finish(pick(0), pick(1), pick(2), pick(3), lam, post,
                                                            sub_ref[...], b64_ref)


def _diff_prompt(scal, cq16, ck16, cv16, sub2, b64, n, tpad):
    nq = tpad // Q_TILE
    qrow = lambda b, i, sc: (b * nq + i, 0)
    seq = lambda b, i, sc: (b, 0)
    const = lambda b, i, sc: (0, 0)
    return pl.pallas_call(
        _diff_prompt_kernel,
        grid_spec=pltpu.PrefetchScalarGridSpec(
            num_scalar_prefetch=1, grid=(n, nq),
            in_specs=[pl.BlockSpec((Q_TILE, W3), qrow), pl.BlockSpec((tpad, W3), seq), pl.BlockSpec((tpad, W3), seq),
                      pl.BlockSpec((1, LANE), const), pl.BlockSpec((LANE, LANE), const)],
            out_specs=pl.BlockSpec((Q_TILE, W3), qrow)),
        out_shape=jax.ShapeDtypeStruct((n * tpad, W3), F32),
        compiler_params=_cparams(("parallel", "arbitrary")),
    )(scal, cq16, ck16, cv16, sub2, b64)


def _idx_sample_kernel(pt_ref, iq_ref, iwb_ref, kidx_ref, iknew_ref, sc_ref, scnew_ref):
    j = pl.program_id(1)
    iq16 = iq_ref[0]
    s = _dot_nt(iq16, kidx_ref[0].astype(BF16))
    row = jnp.sum(iwb_ref[0] * jnp.maximum(s, 0.0), axis=0, keepdims=True)
    sc_ref[0, pl.ds(j, 1), :] = jnp.where(row == 0.0, 0.0, row)

    @pl.when(j == pl.num_programs(1) - 1)
    def _():
        ikn = iknew_ref[0].astype(BF16).astype(F32)
        t = jnp.sum(iq16.astype(F32) * ikn, axis=-1, keepdims=True)
        sn = jnp.sum(iwb_ref[0] * jnp.maximum(t, 0.0), axis=0, keepdims=True)
        scnew_ref[0] = jnp.where(sn == 0.0, 0.0, sn)


def _idx_sample(pt_flat, iq3, iwb, kidx_pages, iknew, layer, pool, ns, npg):
    page = lambda n, j, pt: (layer * pool + pt[n * npg + j], 0, 0)
    per = lambda n, j, pt: (n, 0, 0)
    return pl.pallas_call(
        _idx_sample_kernel,
        grid_spec=pltpu.PrefetchScalarGridSpec(
            num_scalar_prefetch=1, grid=(ns, npg),
            in_specs=[pl.BlockSpec((1, 16, D_IDX), per), pl.BlockSpec((1, 16, LANE), per),
                      pl.BlockSpec((1, PAGE_SIZE, D_IDX), page), pl.BlockSpec((1, 1, D_IDX), per)],
            out_specs=[pl.BlockSpec((1, npg, PAGE_SIZE), per), pl.BlockSpec((1, 1, LANE), per)]),
        out_shape=[jax.ShapeDtypeStruct((ns, npg, PAGE_SIZE), F32), jax.ShapeDtypeStruct((ns, 1, LANE), F32)],
        compiler_params=_cparams(("parallel", "arbitrary")),
    )(pt_flat, iq3, iwb, kidx_pages, iknew)


def _select_sample_kernel(sc_ref, bias_ref, keys_ref, *, top_k, pos_bits):
    nkc = keys_ref.shape[0]
    for c in range(nkc):
        keys_ref[c] = _to_key(sc_ref[c])
    _select_bias(keys_ref, bias_ref, nkc, float(top_k), pos_bits)


def _select_sample(scores3, top_k):
    nkc, rows, kc = scores3.shape
    pos_bits = max(1, int(nkc * kc).bit_length())
    full = pl.BlockSpec((nkc, rows, kc), lambda i: (0, 0, 0))
    return pl.pallas_call(
        functools.partial(_select_sample_kernel, top_k=top_k, pos_bits=pos_bits),
        grid=(1,), in_specs=[full], out_specs=full,
        out_shape=jax.ShapeDtypeStruct((nkc, rows, kc), F32),
        scratch_shapes=[pltpu.VMEM((nkc, rows, kc), I32)],
        compiler_params=_cparams(("arbitrary",)),
    )(scores3)


def _paged_attn_kernel(pt_ref, sc_ref, q_ref, k_ref, v_ref, bias_ref, knew_ref, vnew_ref, biasnew_ref, pick_ref,
                       sub_ref, b64_ref, o_ref, k16_ref, v16_ref, m_ref, l_ref, acc_ref, *, diff):
    j = pl.program_id(1)
    nmap = q_ref.shape[1]

    @pl.when(j == 0)
    def _():
        k16_ref[...] = jnp.zeros_like(k16_ref)
        v16_ref[...] = jnp.zeros_like(v16_ref)
        m_ref[...] = jnp.full_like(m_ref, -jnp.inf)
        l_ref[...] = jnp.zeros_like(l_ref)
        acc_ref[...] = jnp.zeros_like(acc_ref)

    q16 = q_ref[0]
    k16_ref[:, :MIXW] = k_ref[0].astype(BF16)
    v16_ref[:, :MIXW] = v_ref[0].astype(BF16)
    s = _dot_nt(q16, k16_ref[...]) + bias_ref[0, pl.ds(j, 1), :]
    m, l, acc = _online_update((m_ref[...], l_ref[...], acc_ref[...]), s, v16_ref[...])
    m_ref[...], l_ref[...], acc_ref[...] = m, l, acc

    @pl.when(j == pl.num_programs(1) - 1)
    def _():
        rows_new = knew_ref.shape[1]
        k16_ref[:rows_new, :MIXW] = knew_ref[0].astype(BF16)
        v16_ref[:rows_new, :MIXW] = vnew_ref[0].astype(BF16)
        sn = _dot_nt(q16, k16_ref[...]) + biasnew_ref[0]
        mf, lf, accf = _online_update((m, l, acc), sn, v16_ref[...])
        o = accf * (1.0 / lf)
        if diff:
            lam, post = sc_ref[0], sc_ref[1]
            o1 = jnp.sum(o * pick_ref[0], axis=0, keepdims=True)
            o2 = jnp.sum(o * pick_ref[1], axis=0, keepdims=True)
            od = o1 - lam * o2
            for pr in range(3):
                ch = od[:, pr * LANE:(pr + 1) * LANE]
                o_ref[0, :, pr * LANE:(pr + 1) * LANE] = (
                    ch * _seg_rsqrt(ch, b64_ref, V_DIFF) * sub_ref[...] * post)
        else:
            o_ref[0] = jnp.sum(o * pick_ref[0], axis=0, keepdims=True)


def _paged_attn(pt_flat, scal, qmaps, k_pages, v_pages, bias3, knew, vnew, biasnew, pick, sub2, b64,
                layer, pool, ns, npg, diff):
    nmap = qmaps.shape[1]
    rows_new = knew.shape[1]
    page = lambda n, j, pt, sc: (layer * pool + pt[n * npg + j], 0, 0)
    per = lambda n, j, pt, sc: (n, 0, 0)
    c2 = lambda n, j, pt, sc: (0, 0)
    c3 = lambda n, j, pt, sc: (0, 0, 0)
    return pl.pallas_call(
        functools.partial(_paged_attn_kernel, diff=diff),
        grid_spec=pltpu.PrefetchScalarGridSpec(
            num_scalar_prefetch=2, grid=(ns, npg),
            in_specs=[pl.BlockSpec((1, nmap, W3), per),
                      pl.BlockSpec((1, PAGE_SIZE, MIXW), page), pl.BlockSpec((1, PAGE_SIZE, MIXW), page),
                      pl.BlockSpec((1, npg, PAGE_SIZE), per),
                      pl.BlockSpec((1, rows_new, MIXW), per), pl.BlockSpec((1, rows_new, MIXW), per),
                      pl.BlockSpec((1, 1, PAGE_SIZE), per),
                      pl.BlockSpec((2, nmap, W3), c3), pl.BlockSpec((1, LANE), c2), pl.BlockSpec((LANE, LANE), c2)],
            out_specs=pl.BlockSpec((1, 1, W3), per),
            scratch_shapes=[pltpu.VMEM((PAGE_SIZE, W3), BF16), pltpu.VMEM((PAGE_SIZE, W3), BF16),
                            pltpu.VMEM((nmap, 1), F32), pltpu.VMEM((nmap, 1), F32), pltpu.VMEM((nmap, W3), F32)]),
        out_shape=jax.ShapeDtypeStruct((ns, 1, W3), F32),
        compiler_params=_cparams(("parallel", "arbitrary")),
    )(pt_flat, scal, qmaps, k_pages, v_pages, bias3, knew, vnew, biasnew, pick, sub2, b64)


def _outproj_kernel(x_ref, a_ref, d_ref, c_ref, wa_ref, wd_ref, wc_ref, g_ref, wg_ref, wu_ref,
                    x1_ref, gate_ref, up_ref):
    x1 = (x_ref[...] + _dot(a_ref[...].astype(BF16), wa_ref[...]) + _dot(d_ref[...].astype(BF16), wd_ref[...])
          + _dot(c_ref[...].astype(BF16), wc_ref[...]))
    x1_ref[...] = x1
    h = (x1 * lax.rsqrt(jnp.mean(x1 * x1, axis=-1, keepdims=True) + EPS) * g_ref[...]).astype(BF16)
    gate_ref[...] = _dot(h, wg_ref[...])
    up_ref[...] = _dot(h, wu_ref[...])


def _outproj(x2d, a, d, c, wa, wd_, wc, gffn, wg, wu, tm):
    rows, dm = x2d.shape
    dff = wg.shape[1]
    row = lambda i: (i, 0)
    const = lambda i: (0, 0)
    return pl.pallas_call(
        _outproj_kernel, grid=(rows // tm,),
        in_specs=[pl.BlockSpec((tm, dm), row)] + [pl.BlockSpec((tm, W3), row)] * 3
                 + [pl.BlockSpec((W3, dm), const)] * 3
                 + [pl.BlockSpec((1, dm), const), pl.BlockSpec((dm, dff), const), pl.BlockSpec((dm, dff), const)],
        out_specs=[pl.BlockSpec((tm, dm), row), pl.BlockSpec((tm, dff), row), pl.BlockSpec((tm, dff), row)],
        out_shape=[jax.ShapeDtypeStruct((rows, dm), F32), jax.ShapeDtypeStruct((rows, dff), F32),
                   jax.ShapeDtypeStruct((rows, dff), F32)],
        compiler_params=_cparams(("parallel",)),
    )(x2d, a, d, c, wa, wd_, wc, gffn, wg, wu)


def _ffn_tail(gm2, gm1, g0, u, x1, cw_ref, cb_ref, wd_ref):
    gc = cw_ref[0:1, :] * gm2 + cw_ref[1:2, :] * gm1 + cw_ref[2:3, :] * g0 + cb_ref[...]
    return x1 + _dot((_silu(gc) * u).astype(BF16), wd_ref[...])


def _ffn_prompt_kernel(g_ref, halo_ref, u_ref, x1_ref, cw_ref, cb_ref, wd_ref, o_ref, gs_ref):
    i = pl.program_id(1)
    tm = g_ref.shape[0]
    halo = halo_ref[...]
    gs_ref[0:8, :] = jnp.where(i == 0, jnp.zeros_like(halo), halo)
    gs_ref[8:8 + tm, :] = g_ref[...]
    o_ref[...] = _ffn_tail(gs_ref[6:6 + tm, :], gs_ref[7:7 + tm, :], g_ref[...], u_ref[...], x1_ref[...],
                           cw_ref, cb_ref, wd_ref)


def _ffn_prompt(gate, up, x1, cw, cb, wd, n, tpad, tm):
    dff, dm = wd.shape
    nb = tpad // tm
    row = lambda b, i: (b * nb + i, 0)
    halo = lambda b, i: (jnp.maximum((b * nb + i) * (tm // 8) - 1, 0), 0)
    const = lambda b, i: (0, 0)
    return pl.pallas_call(
        _ffn_prompt_kernel, grid=(n, nb),
        in_specs=[pl.BlockSpec((tm, dff), row), pl.BlockSpec((8, dff), halo), pl.BlockSpec((tm, dff), row),
                  pl.BlockSpec((tm, dm), row), pl.BlockSpec((CONV_W, dff), const), pl.BlockSpec((1, dff), const),
                  pl.BlockSpec((dff, dm), const)],
        out_specs=pl.BlockSpec((tm, dm), row),
        out_shape=jax.ShapeDtypeStruct((n * tpad, dm), F32),
        scratch_shapes=[pltpu.VMEM((tm + 8, dff), F32)],
        compiler_params=_cparams(("parallel", "arbitrary")),
    )(gate, gate, up, x1, cw, cb, wd)


def _ffn_sample_kernel(g_ref, s0_ref, s1_ref, u_ref, x1_ref, cw_ref, cb_ref, wd_ref, o_ref):
    o_ref[...] = _ffn_tail(s0_ref[...], s1_ref[...], g_ref[...], u_ref[...], x1_ref[...], cw_ref, cb_ref, wd_ref)


def _ffn_sample(gate, s0, s1, up, x1, cw, cb, wd):
    rows, dff = gate.shape
    dm = wd.shape[1]
    full = lambda shape: pl.BlockSpec(shape, lambda i: (0, 0))
    return pl.pallas_call(
        _ffn_sample_kernel, grid=(1,),
        in_specs=[full((rows, dff))] * 4 + [full((rows, dm)), full((CONV_W, dff)), full((1, dff)), full((dff, dm))],
        out_specs=full((rows, dm)),
        out_shape=jax.ShapeDtypeStruct((rows, dm), F32),
        compiler_params=_cparams(("arbitrary",)),
    )(gate, s0, s1, up, x1, cw, cb, wd)


def _rope_tables(pos, head_dim, rot_dim, theta):
    half = rot_dim // 2
    inv = jnp.exp(-math.log(theta) * jnp.arange(half, dtype=F32) / half)
    ang = pos[:, None] * inv[None, :]
    cos, sin = jnp.cos(ang), jnp.sin(ang)
    t = pos.shape[0]
    rest = head_dim - rot_dim
    c = jnp.concatenate([cos, cos, jnp.ones((t, rest), F32)], axis=1)
    s1 = jnp.concatenate([-sin, jnp.zeros((t, half + rest), F32)], axis=1)
    s2 = jnp.concatenate([jnp.zeros((t, half), F32), sin, jnp.zeros((t, rest), F32)], axis=1)
    rep = LANE // head_dim
    return [jnp.tile(a, (1, rep)) for a in (c, s1, s2)]


def _all_tables(pos):
    return (_rope_tables(pos, D_RET, D_RET, RET_THETA)
            + _rope_tables(pos, D_DSA, D_DSA // ROT_FRAC, ROPE_THETA)
            + _rope_tables(pos, D_DIFF, D_DIFF // ROT_FRAC, ROPE_THETA))


def _block_ones(width):
    i = jnp.arange(LANE)
    return (i[:, None] // width == i[None, :] // width).astype(BF16)


def _pad_cols(a, width):
    return jnp.pad(a, [(0, 0)] * (a.ndim - 1) + [(0, width - a.shape[-1])])


def _prep_w_in(w_in):
    parts, off = [], 0
    for size, width in zip(PROJ_SIZES, GROUP_WIDTHS):
        parts.append(_pad_cols(w_in[..., off:off + size], width))
        off += size
    return jnp.concatenate(parts, axis=-1).astype(BF16)


def _prep_w_out(w_out):
    a = w_out[:, :H_RET * D_RET]
    d = jnp.pad(w_out[:, H_RET * D_RET:H_RET * D_RET + MIXW], ((0, 0), (0, W3 - MIXW), (0, 0)))
    c = jnp.pad(w_out[:, H_RET * D_RET + MIXW:], ((0, 0), (0, W3 - MIXW), (0, 0)))
    return a.astype(BF16), d.astype(BF16), c.astype(BF16)


def _tile_gain(g):
    return jnp.tile(g, (1, LANE // g.shape[-1]))[:, None, :]


def kernel(x_prompt, x_sample, cache_dsa_k, cache_dsa_v, cache_dsa_kidx, cache_diff_k, cache_diff_v, state_ret, state_conv, page_table, meta, norm_mix_g, w_in, ret_gn_g, dsa_qn_g, dsa_kn_g, diff_qn_g, diff_kn_g, diff_lq1, diff_lk1, diff_lq2, diff_lk2, diff_subln_g, w_out, norm_ffn_g, ffn_w_gate, ffn_w_up, ffn_conv_w, ffn_conv_b, ffn_w_down):
    nb, seq, dm = x_prompt.shape
    ns, ts, _ = x_sample.shape
    assert ts == 1
    depth = w_in.shape[0]
    dff = ffn_w_gate.shape[-1]
    pool = cache_dsa_k.shape[1]
    npg = page_table.shape[1]
    past = npg * PAGE_SIZE
    tp = seq + N_META
    tpad = -(-tp // ROW_TILE) * ROW_TILE
    k_p = min(TOPK_MAX, tp // 4)
    k_s = min(TOPK_MAX, (past + ts) // 4)
    tm_p = ROW_TILE
    tm_s = ns
    assert ns % SAMPLE_TILE == 0 and ns % 8 == 0

    w_in_p = _prep_w_in(w_in)
    wo_a, wo_d, wo_c = _prep_w_out(w_out)
    wg16, wu16, wd16 = ffn_w_gate.astype(BF16), ffn_w_up.astype(BF16), ffn_w_down.astype(BF16)
    gmix = norm_mix_g[:, None, :]
    gffn = norm_ffn_g[:, None, :]
    qn2, kn2 = _tile_gain(dsa_qn_g), _tile_gain(dsa_kn_g)
    fqn4, fkn4 = _tile_gain(diff_qn_g), _tile_gain(diff_kn_g)
    sub2 = _tile_gain(diff_subln_g)
    gn_flat = ret_gn_g[:, None, :]
    gn_heads = ret_gn_g.reshape(depth, H_RET, D_RET)
    b64, b32 = _block_ones(64), _block_ones(32)
    tabs_p = _all_tables(jnp.arange(tpad, dtype=F32))
    tabs_s = _all_tables(jnp.full((ns,), float(past), F32))
    cb = ffn_conv_b[:, None, :]

    lam_all = (jnp.exp(jnp.sum(diff_lq1 * diff_lk1, axis=-1)) - jnp.exp(jnp.sum(diff_lq2 * diff_lk2, axis=-1)))

    lane_head = jnp.arange(W3) // V_DIFF
    rows16 = jnp.arange(16)
    pick_dsa = jnp.stack([(rows16[:, None] == lane_head[None, :]), jnp.zeros((16, W3), bool)]).astype(F32)
    pick_diff = jnp.stack([(rows16[:, None] == 2 * lane_head[None, :]),
                           (rows16[:, None] == 2 * lane_head[None, :] + 1)]).astype(F32)

    kidx_pages = cache_dsa_kidx.reshape(depth * pool, PAGE_SIZE, D_IDX)
    dsak_pages = cache_dsa_k.reshape(depth * pool, PAGE_SIZE, MIXW)
    dsav_pages = cache_dsa_v.reshape(depth * pool, PAGE_SIZE, MIXW)
    difk_pages = cache_diff_k.reshape(depth * pool, PAGE_SIZE, MIXW)
    difv_pages = cache_diff_v.reshape(depth * pool, PAGE_SIZE, MIXW)
    state4 = state_ret.reshape(depth * ns, H_RET, D_RET, D_RET)
    pt_flat = page_table.reshape(-1).astype(I32)

    xp = jnp.concatenate([jnp.broadcast_to(meta.astype(x_prompt.dtype), (nb, N_META, dm)), x_prompt], axis=1)
    xp = jnp.pad(xp, ((0, 0), (0, tpad - tp), (0, 0))).reshape(nb * tpad, dm)
    xs = x_sample.reshape(ns, dm)

    outs_p = [[] for _ in range(7)]
    outs_s = [[] for _ in range(7)]
    for l in range(depth):
        lam_init = 0.8 - 0.6 * math.exp(-0.3 * l)
        scal = jnp.stack([lam_all[l] + lam_init, jnp.asarray(1.0 - lam_init, F32)]).astype(F32)
        gains = (qn2[l], kn2[l], fqn4[l], fkn4[l])

        (ret4, bq16, bk16, bv16, iq16, ik16, iw, cq16, ck16, cv16, bk32, bv32, ik32, ck32, cv32) = _inproj(
            xp, gmix[l], w_in_p[l], tabs_p, gains, (b64, b32), tm_p)
        a_p, s_p = _ret_prompt(ret4, gn_flat[l], b64, nb, tpad, tp)
        d_p = _dsa_prompt(bq16, iq16, iw, ik16, bk16, bv16, nb, tpad, k_p)
        c_p = _diff_prompt(scal, cq16, ck16, cv16, sub2[l], b64, nb, tpad)
        x1, gate, up = _outproj(xp, a_p, d_p, c_p, wo_a[l], wo_d[l], wo_c[l], gffn[l], wg16[l], wu16[l], tm_p)
        xp = _ffn_prompt(gate, up, x1, ffn_conv_w[l], cb[l], wd16[l], nb, tpad, tm_p)

        def rows_p(a):
            return a.reshape(nb, tpad, -1)[:, :tp]

        outs_p[0].append(s_p)
        outs_p[1].append(gate.reshape(nb, tpad, dff)[:, tp - (CONV_W - 1):tp])
        outs_p[2].append(rows_p(bk32).reshape(nb, tp, H_DSA, D_DSA))
        outs_p[3].append(rows_p(bv32).reshape(nb, tp, H_DSA, D_DSA))
        outs_p[4].append(rows_p(ik32))
        outs_p[5].append(rows_p(ck32).reshape(nb, tp, H_DIFF, 2 * D_DIFF))
        outs_p[6].append(rows_p(cv32).reshape(nb, tp, H_DIFF, V_DIFF))

        (ret4, bq16, bk16, bv16, iq16, ik16, iw, cq16, ck16, cv16, bk32, bv32, ik32, ck32, cv32) = _inproj(
            xs, gmix[l], w_in_p[l], tabs_s, gains, (b64, b32), tm_s)
        nblk = ns // SAMPLE_TILE
        to_cols = lambda a: a.reshape(nblk, SAMPLE_TILE, W3).transpose(0, 2, 1)
        a3, s_s = _ret_sample(to_cols(ret4[:, O_RQ:O_RQ + W3]), to_cols(ret4[:, O_RK:O_RK + W3]),
                              ret4[:, O_RV:O_RV + W3].reshape(ns, H_RET, D_RET),
                              ret4[:, O_RG:O_RG + W3].reshape(ns, H_RET, D_RET), gn_heads[l], state4, l, ns)
        a_s = a3.reshape(ns, W3)
        iq3 = jnp.pad(iq16.reshape(ns, H_IDX, D_IDX), ((0, 0), (0, 16 - H_IDX), (0, 0)))
        iwb = jnp.pad(jnp.broadcast_to(iw[:, :H_IDX, None], (ns, H_IDX, LANE)), ((0, 0), (0, 16 - H_IDX), (0, 0)))
        sc_pages, sc_new = _idx_sample(pt_flat, iq3, iwb, kidx_pages, ik32[:, None, :], l, pool, ns, npg)
        n_keys = past + ts
        kc_s = K_CHUNK
        nkc_s = -(-n_keys // kc_s)
        scores = jnp.concatenate([sc_pages.reshape(ns, past), sc_new[:, 0, :1],
                                  jnp.full((ns, nkc_s * kc_s - n_keys), NEG, F32)], axis=1)
        bias = _select_sample(scores.reshape(ns, nkc_s, kc_s).transpose(1, 0, 2), k_s)
        bias = bias.transpose(1, 0, 2).reshape(ns, nkc_s * kc_s)
        bias_pages = bias[:, :past].reshape(ns, npg, PAGE_SIZE)
        neg_tail = jnp.full((ns, PAGE_SIZE - 1), NEG, F32)
        bias_new = jnp.concatenate([bias[:, past:past + 1], neg_tail], axis=1)[:, None, :]
        q_dsa = jnp.where(pick_dsa[0][None] > 0, bq16[:, None, :], 0).astype(BF16)
        pad8 = lambda a: jnp.pad(a[:, None, :], ((0, 0), (0, 15), (0, 0)))
        d_s = _paged_attn(pt_flat, scal, q_dsa, dsak_pages, dsav_pages, bias_pages, pad8(bk32), pad8(bv32), bias_new,
                          pick_dsa, sub2[l], b64, l, pool, ns, npg, False).reshape(ns, W3)
        lane_sub = (jnp.arange(W3) % V_DIFF) // D_DIFF
        pick_q = (rows16[:, None] == (2 * lane_head + lane_sub)[None, :])
        q_dif = jnp.where(pick_q[None], cq16[:, None, :], 0).astype(BF16)
        zero_pages = jnp.zeros((ns, npg, PAGE_SIZE), F32)
        zero_new = jnp.concatenate([jnp.zeros((ns, 1), F32), neg_tail], axis=1)[:, None, :]
        c_s = _paged_attn(pt_flat, scal, q_dif, difk_pages, difv_pages, zero_pages, pad8(ck32), pad8(cv32), zero_new,
                          pick_diff, sub2[l], b64, l, pool, ns, npg, True).reshape(ns, W3)
        x1, gate, up = _outproj(xs, a_s, d_s, c_s, wo_a[l], wo_d[l], wo_c[l], gffn[l], wg16[l], wu16[l], tm_s)
        xs = _ffn_sample(gate, state_conv[l, :, 0], state_conv[l, :, 1], up, x1, ffn_conv_w[l], cb[l], wd16[l])

        outs_s[0].append(s_s)
        outs_s[1].append(jnp.stack([state_conv[l, :, 1], gate], axis=1))
        outs_s[2].append(bk32.reshape(ns, ts, H_DSA, D_DSA))
        outs_s[3].append(bv32.reshape(ns, ts, H_DSA, D_DSA))
        outs_s[4].append(ik32.reshape(ns, ts, D_IDX))
        outs_s[5].append(ck32.reshape(ns, ts, H_DIFF, 2 * D_DIFF))
        outs_s[6].append(cv32.reshape(ns, ts, H_DIFF, V_DIFF))

    ret_p, conv_p, dsak_p, dsav_p, kidx_p, diffk_p, diffv_p = (jnp.stack(a) for a in outs_p)
    ret_s, conv_s, dsak_s, dsav_s, kidx_s, diffk_s, diffv_s = (jnp.stack(a) for a in outs_s)
    y_prompt = xp.reshape(nb, tpad, dm)[:, N_META:tp]
    y_sample = xs.reshape(ns, ts, dm)
    return (y_prompt, y_sample, ret_p, ret_s, conv_p, conv_s, dsak_p, dsak_s, dsav_p, dsav_s,
            kidx_p, kidx_s, diffk_p, diffk_s, diffv_p, diffv_s)
```

```python
import functools
import math
import struct

import jax
import jax.numpy as jnp
from jax import lax
from jax.experimental import pallas as pl
from jax.experimental.pallas import tpu as pltpu

F32 = jnp.float32
BF16 = jnp.bfloat16
I32 = jnp.int32

N_META = 16
H_RET, D_RET = 6, 64
H_DSA, D_DSA = 5, 64
H_IDX, D_IDX = 8, 64
TOPK_MAX = 256
H_DIFF, D_DIFF, V_DIFF = 5, 32, 64
CONV_W = 3
ROPE_THETA = 500000.0
RET_THETA = 10000.0
ROT_FRAC = 4
PAGE_SIZE = 128
NEG = -1e30
EPS = 1e-6
LOG2E = math.log2(math.e)
PROJ_SIZES = ((H_RET * D_RET,) * 4 + (H_DSA * D_DSA,) * 3 + (H_IDX * D_IDX, D_IDX, H_IDX)
              + (2 * H_DIFF * D_DIFF, 2 * H_DIFF * D_DIFF, H_DIFF * V_DIFF))

LANE = 128
VMEM_LIMIT = 56 * 1024 * 1024

W3 = 3 * LANE
GROUP_WIDTHS = (W3, W3, W3, W3, W3, W3, W3, H_IDX * D_IDX, LANE, LANE, W3, W3, W3)
GROUP_OFFS = tuple(int(sum(GROUP_WIDTHS[:i])) for i in range(len(GROUP_WIDTHS)))
IN_PAD = int(sum(GROUP_WIDTHS))
(O_RQ, O_RK, O_RV, O_RG, O_BQ, O_BK, O_BV, O_IQ, O_IK, O_IW, O_CQ, O_CK, O_CV) = GROUP_OFFS
MIXW = H_DSA * D_DSA

ROW_TILE = 256
Q_TILE = 128
K_CHUNK = 512
RET_CHUNK = 128
SAMPLE_TILE = 16


def _float_key(v):
    b = struct.unpack("<i", struct.pack("<f", v))[0]
    return b if b >= 0 else b ^ 0x7FFFFFFF


NEG_KEY = _float_key(NEG)
INT_MAX = 2 ** 31 - 1
INT_MIN = -2 ** 31


def _cparams(sem):
    return pltpu.CompilerParams(dimension_semantics=sem, vmem_limit_bytes=VMEM_LIMIT)


def _seq_spec(shape, index_map):
    return pl.BlockSpec(shape, index_map, pipeline_mode=pl.Buffered(1))


def _dot(a, b):
    return jnp.dot(a, b, preferred_element_type=F32)


def _dot_nt(a, b):
    return lax.dot_general(a, b, (((1,), (1,)), ((), ())), preferred_element_type=F32)


def _silu(x):
    return x * (1.0 / (1.0 + jnp.exp(-x)))


def _seg_rsqrt(x, seg_ref, width):
    x2 = x * x
    hi = x2.astype(BF16)
    lo = (x2 - hi.astype(F32)).astype(BF16)
    seg = seg_ref[...]
    ss = _dot(hi, seg) + _dot(lo, seg)
    return lax.rsqrt(ss * (1.0 / width) + EPS)


def _rope_chunk(x, c, s1, s2, half):
    return x * c + pltpu.roll(x, LANE - half, 1) * s1 + pltpu.roll(x, half, 1) * s2


def _lane_mask(shape, lo, hi):
    lane = lax.broadcasted_iota(I32, shape, len(shape) - 1)
    return (lane >= lo) & (lane < hi)


def _inproj_kernel(x_ref, g_ref, w_ref, rc_ref, rs1_ref, rs2_ref, dc_ref, ds1_ref, ds2_ref,
                   fc_ref, fs1_ref, fs2_ref, qn_ref, kn_ref, fqn_ref, fkn_ref, b64_ref, b32_ref,
                   ret4_ref, bq16_ref, bk16_ref, bv16_ref, iq16_ref, ik16_ref, iw_ref,
                   cq16_ref, ck16_ref, cv16_ref, bk32_ref, bv32_ref, ik32_ref, ck32_ref, cv32_ref):
    x = x_ref[...]
    h = x * lax.rsqrt(jnp.mean(x * x, axis=-1, keepdims=True) + EPS) * g_ref[...]
    hb = h.astype(BF16)

    def proj(off, j):
        return _dot(hb, w_ref[:, off + j * LANE: off + (j + 1) * LANE])

    ret_tab = (rc_ref[...], rs1_ref[...], rs2_ref[...], D_RET // 2)
    dsa_tab = (dc_ref[...], ds1_ref[...], ds2_ref[...], D_DSA // ROT_FRAC // 2)
    dif_tab = (fc_ref[...], fs1_ref[...], fs2_ref[...], D_DIFF // ROT_FRAC // 2)

    def rope(p, tab):
        return _rope_chunk(p, *tab)

    for j in range(3):
        sl = slice(j * LANE, (j + 1) * LANE)
        ret4_ref[:, O_RQ + j * LANE: O_RQ + (j + 1) * LANE] = rope(proj(O_RQ, j), ret_tab)
        ret4_ref[:, O_RK + j * LANE: O_RK + (j + 1) * LANE] = rope(proj(O_RK, j), ret_tab) * (D_RET ** -0.5)
        ret4_ref[:, O_RV + j * LANE: O_RV + (j + 1) * LANE] = proj(O_RV, j)
        ret4_ref[:, O_RG + j * LANE: O_RG + (j + 1) * LANE] = proj(O_RG, j)
        p = proj(O_BQ, j)
        bq = rope(p * _seg_rsqrt(p, b64_ref, D_DSA) * qn_ref[...], dsa_tab) * (D_DSA ** -0.5 * LOG2E)
        bq16_ref[:, sl] = bq.astype(BF16)
        p = proj(O_BK, j)
        bk = rope(p * _seg_rsqrt(p, b64_ref, D_DSA) * kn_ref[...], dsa_tab)
        bk16_ref[:, sl] = bk.astype(BF16)
        bv = proj(O_BV, j)
        bv16_ref[:, sl] = bv.astype(BF16)
        p = proj(O_CQ, j)
        cq = rope(p * _seg_rsqrt(p, b32_ref, D_DIFF) * fqn_ref[...], dif_tab) * (D_DIFF ** -0.5 * LOG2E)
        cq16_ref[:, sl] = cq.astype(BF16)
        p = proj(O_CK, j)
        ck = rope(p * _seg_rsqrt(p, b32_ref, D_DIFF) * fkn_ref[...], dif_tab)
        ck16_ref[:, sl] = ck.astype(BF16)
        cv = proj(O_CV, j)
        cv16_ref[:, sl] = cv.astype(BF16)
        w = LANE if j < 2 else MIXW - 2 * LANE
        bk32_ref[:, j * LANE: j * LANE + w] = bk[:, :w]
        bv32_ref[:, j * LANE: j * LANE + w] = bv[:, :w]
        ck32_ref[:, j * LANE: j * LANE + w] = ck[:, :w]
        cv32_ref[:, j * LANE: j * LANE + w] = cv[:, :w]
    for j in range(H_IDX * D_IDX // LANE):
        iq16_ref[:, j * LANE:(j + 1) * LANE] = rope(proj(O_IQ, j), dsa_tab).astype(BF16)
    ik = rope(proj(O_IK, 0), dsa_tab)
    ik32_ref[...] = ik[:, :D_IDX]
    ik16_ref[...] = jnp.where(_lane_mask(ik.shape, 0, D_IDX), ik, pltpu.roll(ik, D_IDX, 1)).astype(BF16)
    iw_ref[...] = proj(O_IW, 0) * ((H_IDX * D_IDX) ** -0.5)


def _inproj(x2d, gmix, w, tabs, gains, segs, tm):
    rows, dm = x2d.shape
    tab_rows = tabs[0].shape[0]
    nb_tab = tab_rows // tm
    grid = (rows // tm,)
    row = lambda i: (i, 0)
    const = lambda i: (0, 0)
    tab = lambda i: (i % nb_tab, 0)
    in_specs = ([pl.BlockSpec((tm, dm), row), pl.BlockSpec((1, dm), const), pl.BlockSpec((dm, IN_PAD), const)]
                + [pl.BlockSpec((tm, LANE), tab)] * 9
                + [pl.BlockSpec((1, LANE), const)] * 4
                + [pl.BlockSpec((LANE, LANE), const)] * 2)
    widths = [(4 * W3, F32), (W3, BF16), (W3, BF16), (W3, BF16), (H_IDX * D_IDX, BF16), (LANE, BF16), (LANE, F32),
              (W3, BF16), (W3, BF16), (W3, BF16), (MIXW, F32), (MIXW, F32), (D_IDX, F32), (MIXW, F32), (MIXW, F32)]
    out_shape = [jax.ShapeDtypeStruct((rows, wd), dt) for wd, dt in widths]
    out_specs = [pl.BlockSpec((tm, wd), row) for wd, _ in widths]
    return pl.pallas_call(
        _inproj_kernel, grid=grid, in_specs=in_specs, out_specs=out_specs, out_shape=out_shape,
        compiler_params=_cparams(("parallel",)),
    )(x2d, gmix, w, *tabs, *gains, *segs)


def _ret_lg(h):
    return math.log(1.0 - 2.0 ** (-5.0 - h))


def _gate(o, g, gn, b64_ref):
    return _silu(g) * (o * _seg_rsqrt(o, b64_ref, D_RET) * gn)


def _ret_prompt_kernel(ret4_ref, gn_ref, b64_ref, a_ref, sfin_ref, s_ref, *, tp):
    c = pl.program_id(1)
    ch = RET_CHUNK

    @pl.when(c == 0)
    def _():
        s_ref[...] = jnp.zeros_like(s_ref)

    valid = jnp.clip(tp - c * ch, 0, ch).astype(F32)
    ri = lax.broadcasted_iota(I32, (ch, 1), 0).astype(F32)
    rel = (lax.broadcasted_iota(I32, (ch, ch), 0) - lax.broadcasted_iota(I32, (ch, ch), 1)).astype(F32)
    lane = lax.broadcasted_iota(I32, (1, LANE), 1)
    rowi = lax.broadcasted_iota(I32, (LANE, 1), 0)
    for p in range(H_RET // 2):
        sl = slice(p * LANE, (p + 1) * LANE)
        lg_a, lg_b = _ret_lg(2 * p), _ret_lg(2 * p + 1)
        q = ret4_ref[:, O_RQ + p * LANE: O_RQ + (p + 1) * LANE]
        k = ret4_ref[:, O_RK + p * LANE: O_RK + (p + 1) * LANE]
        v = ret4_ref[:, O_RV + p * LANE: O_RV + (p + 1) * LANE]
        k16 = k.astype(BF16)
        v16 = v.astype(BF16)
        s_pair = s_ref[p]
        s16 = s_pair.astype(BF16)
        outs = []
        for half, lg in ((0, lg_a), (1, lg_b)):
            qm = jnp.where(_lane_mask(q.shape, half * D_RET, (half + 1) * D_RET), q, 0.0).astype(BF16)
            dmat = jnp.where(rel >= 0.0, jnp.exp(jnp.maximum(rel, 0.0) * lg), 0.0)
            inner = _dot_nt(qm, k16) * dmat
            outs.append(_dot(inner.astype(BF16), v16) + _dot(qm, s16) * jnp.exp((ri + 1.0) * lg))
        o_pair = jnp.where(lane < D_RET, outs[0], outs[1])
        lgv = jnp.where(lane < D_RET, lg_a, lg_b)
        kw = k * jnp.where(ri < valid, jnp.exp((valid - 1.0 - ri) * lgv), 0.0)
        lgc = jnp.where(rowi < D_RET, lg_a, lg_b)
        s_new = jnp.exp(valid * lgc) * s_pair + _dot(kw.T.astype(BF16), v16)
        s_ref[p] = s_new
        g = ret4_ref[:, O_RG + p * LANE: O_RG + (p + 1) * LANE]
        a_ref[:, sl] = _gate(o_pair, g, gn_ref[:, sl], b64_ref)

    @pl.when(c == pl.num_programs(1) - 1)
    def _():
        for h in range(H_RET):
            blk = s_ref[h // 2]
            o = (h % 2) * D_RET
            sfin_ref[0, h] = blk[o:o + D_RET, o:o + D_RET]


def _ret_prompt(ret4, gn, b64, n, tpad, tp):
    nc = tpad // RET_CHUNK
    return pl.pallas_call(
        functools.partial(_ret_prompt_kernel, tp=tp),
        grid=(n, nc),
        in_specs=[pl.BlockSpec((RET_CHUNK, 4 * W3), lambda b, c: (b * nc + c, 0)),
                  pl.BlockSpec((1, W3), lambda b, c: (0, 0)),
                  pl.BlockSpec((LANE, LANE), lambda b, c: (0, 0))],
        out_specs=[pl.BlockSpec((RET_CHUNK, W3), lambda b, c: (b * nc + c, 0)),
                   pl.BlockSpec((1, H_RET, D_RET, D_RET), lambda b, c: (b, 0, 0, 0))],
        out_shape=[jax.ShapeDtypeStruct((n * tpad, W3), F32),
                   jax.ShapeDtypeStruct((n, H_RET, D_RET, D_RET), F32)],
        scratch_shapes=[pltpu.VMEM((H_RET // 2, LANE, LANE), F32)],
        compiler_params=_cparams(("parallel", "arbitrary")),
    )(ret4, gn, b64)


def _ret_sample_kernel(qt_ref, kt_ref, v_ref, g_ref, gn_ref, st_ref, a_ref, snew_ref, o_ref):
    nb = v_ref.shape[0]
    for j in range(nb):
        for h in range(H_RET):
            gamma = math.exp(_ret_lg(h))
            qcol = qt_ref[0, h * D_RET:(h + 1) * D_RET, j:j + 1]
            kcol = kt_ref[0, h * D_RET:(h + 1) * D_RET, j:j + 1]
            vrow = v_ref[j, h:h + 1, :]
            s_new = gamma * st_ref[j, h] + kcol * vrow
            snew_ref[j, h] = s_new
            o_ref[j, h:h + 1, :] = jnp.sum(qcol * s_new, axis=0, keepdims=True)
    o = o_ref[...]
    r = lax.rsqrt(jnp.mean(o * o, axis=-1, keepdims=True) + EPS)
    a_ref[...] = _silu(g_ref[...]) * (o * r * gn_ref[...])


def _ret_sample(qt, kt, v3, g3, gn2, state4, layer, ns):
    nb = qt.shape[2]
    nblk = ns // nb
    blk3 = pl.BlockSpec((nb, H_RET, D_RET), lambda i: (i, 0, 0))
    tsp = pl.BlockSpec((1, H_RET * D_RET, nb), lambda i: (i, 0, 0))
    return pl.pallas_call(
        _ret_sample_kernel,
        grid=(nblk,),
        in_specs=[tsp, tsp, blk3, blk3,
                  pl.BlockSpec((H_RET, D_RET), lambda i: (0, 0)),
                  pl.BlockSpec((nb, H_RET, D_RET, D_RET), lambda i: (layer * nblk + i, 0, 0, 0))],
        out_specs=[blk3, pl.BlockSpec((nb, H_RET, D_RET, D_RET), lambda i: (i, 0, 0, 0))],
        out_shape=[jax.ShapeDtypeStruct((ns, H_RET, D_RET), F32),
                   jax.ShapeDtypeStruct((ns, H_RET, D_RET, D_RET), F32)],
        scratch_shapes=[pltpu.VMEM((nb, H_RET, D_RET), F32)],
        compiler_params=_cparams(("parallel",)),
    )(qt, kt, v3, g3, gn2, state4)


def _to_key(s):
    b = lax.bitcast_convert_type(s, I32)
    return jnp.where(b >= 0, b, b ^ jnp.int32(0x7FFFFFFF))


def _fold_lanes(x):
    acc = x[:, :LANE]
    for t in range(1, x.shape[1] // LANE):
        acc = acc + x[:, t * LANE:(t + 1) * LANE]
    return acc


def _select_bias(keys_ref, bias_ref, nkc, k, pos_bits):
    _, rows, kc = keys_ref.shape
    groups = kc // LANE
    lane_pos = lax.broadcasted_iota(I32, (rows, LANE), 1)
    wide = lambda col: jnp.broadcast_to(col, (rows, LANE))

    def count(ones):
        def body(c, acc):
            for g in range(groups):
                acc = acc + ones(keys_ref[c, :, g * LANE:(g + 1) * LANE], c * kc + g * LANE + lane_pos)
            return acc
        acc = lax.fori_loop(0, nkc, body, jnp.zeros((rows, LANE), F32))
        return jnp.sum(acc, axis=-1, keepdims=True)

    def thr_step(it, thr):
        cand = thr + lax.shift_left(jnp.int32(1), 31 - it)
        cand_b = wide(cand)
        tot = count(lambda kk, kpos: jnp.where(kk >= cand_b, 1.0, 0.0))
        return jnp.where(tot >= k, cand, thr)

    thr = lax.fori_loop(0, 32, thr_step, jnp.full((rows, 1), INT_MIN, I32))
    thr_b = wide(thr)
    c_gt = count(lambda kk, kpos: jnp.where(kk > thr_b, 1.0, 0.0))
    c_ge = count(lambda kk, kpos: jnp.where(kk >= thr_b, 1.0, 0.0))
    need = k - c_gt
    excess = jnp.where(thr > NEG_KEY, c_ge - k, 0.0)

    def tie_search():
        def step(it, lo):
            cand = lo + lax.shift_left(jnp.int32(1), pos_bits - 1 - it)
            cand_b = wide(cand)
            f = count(lambda kk, kpos: jnp.where(kk == thr_b, jnp.where(kpos < cand_b, 1.0, 0.0), 0.0))
            return jnp.where(f < need, cand, lo)
        return lax.fori_loop(0, pos_bits, step, jnp.zeros((rows, 1), I32))

    last = lax.cond(jnp.max(excess) > 0.0, tie_search, lambda: jnp.full((rows, 1), INT_MAX, I32))
    last_b = wide(last)

    def emit(c, carry):
        for g in range(groups):
            cols = slice(g * LANE, (g + 1) * LANE)
            kk = keys_ref[c, :, cols]
            tie = jnp.where(c * kc + g * LANE + lane_pos <= last_b, 0.0, NEG)
            sel = jnp.where(kk > thr_b, 0.0, jnp.where(kk == thr_b, tie, NEG))
            bias_ref[c, :, cols] = jnp.where(kk > NEG_KEY, sel, NEG)
        return carry

    lax.fori_loop(0, nkc, emit, 0)


def _fold_max(x):
    acc = x[:, :LANE]
    for t in range(1, x.shape[1] // LANE):
        acc = jnp.maximum(acc, x[:, t * LANE:(t + 1) * LANE])
    return acc


def _softmax_pv(s_ref, n_chunks, maccs, v_chunks):
    nmap, _, rows, kc = s_ref.shape
    mbs = [jnp.broadcast_to(jnp.max(m, axis=-1, keepdims=True), (rows, LANE)) for m in maccs]

    def pass_pv(c, carry):
        new = []
        for i in range(nmap):
            lacc, acc = carry[i]
            ps = [jnp.exp2(s_ref[i, c, :, g * LANE:(g + 1) * LANE] - mbs[i]) for g in range(kc // LANE)]
            for p in ps:
                lacc = lacc + p
            acc = acc + _dot(jnp.concatenate(ps, axis=1).astype(BF16), v_chunks[i](c))
            new.append((lacc, acc))
        return tuple(new)

    zero = jnp.zeros((rows, LANE), F32)
    fin = lax.fori_loop(0, n_chunks, pass_pv, tuple((zero, zero) for _ in range(nmap)))
    return [(acc, jnp.sum(lacc, axis=-1, keepdims=True)) for lacc, acc in fin]


def _online_update(carry, s, v16):
    m, l, acc = carry
    m_new = jnp.maximum(m, jnp.max(s, axis=-1, keepdims=True))
    alpha = jnp.exp2(m - m_new)
    p = jnp.exp2(s - m_new)
    l = alpha * l + jnp.sum(p, axis=-1, keepdims=True)
    acc = alpha * acc + _dot(p.astype(BF16), v16)
    return m_new, l, acc


def _online_init(rows, width):
    return (jnp.full((rows, 1), -jnp.inf, F32), jnp.zeros((rows, 1), F32), jnp.zeros((rows, width), F32))


def _dsa_prompt_kernel(bq_ref, iq_ref, iw_ref, ik_ref, bk_ref, bv_ref, o_ref, keys_ref, bias_ref, s_ref,
                       *, top_k, pos_bits, tp):
    q0 = pl.program_id(1) * Q_TILE

    @pl.when(q0 < tp)
    def _():
        _dsa_prompt_tile(bq_ref, iq_ref, iw_ref, ik_ref, bk_ref, bv_ref, o_ref, keys_ref, bias_ref, s_ref,
                         q0, top_k, pos_bits)

    @pl.when(q0 >= tp)
    def _():
        o_ref[...] = jnp.zeros_like(o_ref)


def _dsa_prompt_tile(bq_ref, iq_ref, iw_ref, ik_ref, bk_ref, bv_ref, o_ref, keys_ref, bias_ref, s_ref,
                     q0, top_k, pos_bits):
    tq, kc = Q_TILE, K_CHUNK
    nkc = (q0 + tq + kc - 1) // kc
    qpos = q0 + lax.broadcasted_iota(I32, (tq, 1), 0)
    iw = iw_ref[...]

    iq_heads = []
    for h in range(H_IDX):
        ch = iq_ref[:, (h // 2) * LANE:(h // 2 + 1) * LANE]
        iq_heads.append(jnp.where(_lane_mask(ch.shape, (h % 2) * D_IDX, (h % 2 + 1) * D_IDX), ch, 0).astype(BF16))

    def score_body(c, carry):
        kch = ik_ref[pl.ds(pl.multiple_of(c * kc, kc), kc), :]
        acc = jnp.zeros((tq, kc), F32)
        for h in range(H_IDX):
            acc = acc + iw[:, h:h + 1] * jnp.maximum(_dot_nt(iq_heads[h], kch), 0.0)
        acc = jnp.where(acc == 0.0, 0.0, acc)
        kpos = c * kc + lax.broadcasted_iota(I32, (1, kc), 1)
        keys_ref[c] = _to_key(jnp.where(kpos <= qpos, acc, NEG))
        return carry

    lax.fori_loop(0, nkc, score_body, 0)
    _select_bias(keys_ref, bias_ref, nkc, float(top_k), pos_bits)

    lane = lax.broadcasted_iota(I32, (1, LANE), 1)
    qms = []
    for h in range(H_DSA):
        ch = bq_ref[:, (h // 2) * LANE:(h // 2 + 1) * LANE]
        qms.append(jnp.where(_lane_mask(ch.shape, (h % 2) * D_DSA, (h % 2 + 1) * D_DSA), ch, 0).astype(BF16))

    head_cols = [slice((h // 2) * LANE, (h // 2 + 1) * LANE) for h in range(H_DSA)]

    def pass_qk(c, maccs):
        rows = pl.ds(pl.multiple_of(c * kc, kc), kc)
        bias = bias_ref[c]
        new = []
        for h in range(H_DSA):
            s = _dot_nt(qms[h], bk_ref[rows, head_cols[h]]) + bias
            s_ref[h, c] = s
            new.append(jnp.maximum(maccs[h], _fold_max(s)))
        return tuple(new)

    maccs = lax.fori_loop(0, nkc, pass_qk, tuple(jnp.full((tq, LANE), -jnp.inf, F32) for _ in range(H_DSA)))
    v_chunks = [lambda c, cols=cols: bv_ref[pl.ds(pl.multiple_of(c * kc, kc), kc), cols] for cols in head_cols]
    outs = [acc * (1.0 / l) for acc, l in _softmax_pv(s_ref, nkc, maccs, v_chunks)]
    outs.append(jnp.zeros_like(outs[0]))
    for pr in range(3):
        o_ref[:, pr * LANE:(pr + 1) * LANE] = jnp.where(lane < D_DSA, outs[2 * pr], outs[2 * pr + 1])


def _dsa_prompt(bq16, iq16, iw, ik16, bk16, bv16, n, tpad, tp, top_k):
    nq = tpad // Q_TILE
    nkc = tpad // K_CHUNK
    pos_bits = max(1, int(tpad).bit_length())
    qrow = lambda b, i: (b * nq + i, 0)
    seq = lambda b, i: (b, 0)
    return pl.pallas_call(
        functools.partial(_dsa_prompt_kernel, top_k=top_k, pos_bits=pos_bits, tp=tp),
        grid=(n, nq),
        in_specs=[pl.BlockSpec((Q_TILE, W3), qrow), pl.BlockSpec((Q_TILE, H_IDX * D_IDX), qrow),
                  pl.BlockSpec((Q_TILE, LANE), qrow),
                  _seq_spec((tpad, LANE), seq), _seq_spec((tpad, W3), seq), _seq_spec((tpad, W3), seq)],
        out_specs=pl.BlockSpec((Q_TILE, W3), qrow),
        out_shape=jax.ShapeDtypeStruct((n * tpad, W3), F32),
        scratch_shapes=[pltpu.VMEM((nkc, Q_TILE, K_CHUNK), I32), pltpu.VMEM((nkc, Q_TILE, K_CHUNK), F32),
                        pltpu.VMEM((H_DSA, nkc, Q_TILE, K_CHUNK), F32)],
        compiler_params=_cparams(("parallel", "arbitrary")),
    )(bq16, iq16, iw, ik16, bk16, bv16)


def _diff_finish(o1, o2, l1, l2, lam, post, sub, b64_ref):
    o = o1 * (1.0 / l1) - lam * (o2 * (1.0 / l2))
    return o * _seg_rsqrt(o, b64_ref, V_DIFF) * sub * post


def _diff_prompt_kernel(sc_ref, cq_ref, ck_ref, cv_ref, sub_ref, b64_ref, o_ref, s_ref, *, tp):
    q0 = pl.program_id(1) * Q_TILE

    @pl.when(q0 < tp)
    def _():
        _diff_prompt_tile(sc_ref, cq_ref, ck_ref, cv_ref, sub_ref, b64_ref, o_ref, s_ref, q0)

    @pl.when(q0 >= tp)
    def _():
        o_ref[...] = jnp.zeros_like(o_ref)


def _diff_prompt_tile(sc_ref, cq_ref, ck_ref, cv_ref, sub_ref, b64_ref, o_ref, s_ref, q0):
    tq, kc = Q_TILE, K_CHUNK
    nfull = q0 // kc
    qpos = q0 + lax.broadcasted_iota(I32, (tq, 1), 0)
    lam, post = sc_ref[0], sc_ref[1]
    lane = lax.broadcasted_iota(I32, (1, LANE), 1)
    qmaps = []
    for r in range(2 * H_DIFF):
        ch = cq_ref[:, (r // 4) * LANE:(r // 4 + 1) * LANE]
        base = (r % 4) * D_DIFF
        qmaps.append(jnp.where(_lane_mask(ch.shape, base, base + D_DIFF), ch, 0).astype(BF16))

    ok = (nfull * kc + lax.broadcasted_iota(I32, (1, kc), 1)) <= qpos
    fin = []
    for grp in range(2):
        maps = range(grp * H_DIFF, (grp + 1) * H_DIFF)
        map_cols = [slice((r // 4) * LANE, (r // 4 + 1) * LANE) for r in maps]

        def pass_qk(c, maccs, maps=maps, map_cols=map_cols, masked=False):
            rows = pl.ds(pl.multiple_of(c * kc, kc), kc)
            new = []
            for i, r in enumerate(maps):
                s = _dot_nt(qmaps[r], ck_ref[rows, map_cols[i]])
                if masked:
                    s = jnp.where(ok, s, NEG)
                s_ref[i, c] = s
                new.append(jnp.maximum(maccs[i], _fold_max(s)))
            return tuple(new)

        maccs = lax.fori_loop(0, nfull, pass_qk, tuple(jnp.full((tq, LANE), -jnp.inf, F32) for _ in maps))
        maccs = pass_qk(nfull, maccs, masked=True)
        v_chunks = [lambda c, cols=cols: cv_ref[pl.ds(pl.multiple_of(c * kc, kc), kc), cols] for cols in map_cols]
        fin.extend(_softmax_pv(s_ref, nfull + 1, maccs, v_chunks))
    res = [(fin[2 * h][0], fin[2 * h + 1][0], fin[2 * h][1], fin[2 * h + 1][1]) for h in range(H_DIFF)]
    res.append(res[-1])
    for pr in range(3):
        ra, rb = res[2 * pr], res[2 * pr + 1]
        pick = lambda t: jnp.where(lane < V_DIFF, ra[t], rb[t])
        o_ref[:, pr * LANE:(pr + 1) * LANE] = _diff_finish(pick(0), pick(1), pick(2), pick(3), lam, post,
                                                            sub_ref[...], b64_ref)


def _diff_prompt(scal, cq16, ck16, cv16, sub2, b64, n, tpad, tp):
    nq = tpad // Q_TILE
    qrow = lambda b, i, sc: (b * nq + i, 0)
    seq = lambda b, i, sc: (b, 0)
    const = lambda b, i, sc: (0, 0)
    return pl.pallas_call(
        functools.partial(_diff_prompt_kernel, tp=tp),
        grid_spec=pltpu.PrefetchScalarGridSpec(
            num_scalar_prefetch=1, grid=(n, nq),
            in_specs=[pl.BlockSpec((Q_TILE, W3), qrow), _seq_spec((tpad, W3), seq), _seq_spec((tpad, W3), seq),
                      pl.BlockSpec((1, LANE), const), pl.BlockSpec((LANE, LANE), const)],
            out_specs=pl.BlockSpec((Q_TILE, W3), qrow),
            scratch_shapes=[pltpu.VMEM((H_DIFF, tpad // K_CHUNK, Q_TILE, K_CHUNK), F32)]),
        out_shape=jax.ShapeDtypeStruct((n * tpad, W3), F32),
        compiler_params=_cparams(("parallel", "arbitrary")),
    )(scal, cq16, ck16, cv16, sub2, b64)


def _page_group(npg):
    return next(g for g in (8, 4, 2, 1) if npg % g == 0)


def _idx_sample_kernel(pt_ref, iq_ref, iwb_ref, *refs, group):
    kidx_refs = refs[:group]
    iknew_ref, sc_ref, scnew_ref = refs[group:]
    j = pl.program_id(1)
    iq16 = iq_ref[0]
    iw_col = iwb_ref[0][:, 0:1]
    rows = []
    for g in range(group):
        s = _dot_nt(iq16, kidx_refs[g][0].astype(BF16))
        rows.append(jnp.sum(iw_col * jnp.maximum(s, 0.0), axis=0, keepdims=True))
    row = jnp.concatenate(rows, axis=1)
    sc_ref[0, pl.ds(j, 1), :] = jnp.where(row == 0.0, 0.0, row)

    @pl.when(j == pl.num_programs(1) - 1)
    def _():
        ikn = iknew_ref[0].astype(BF16).astype(F32)
        t = jnp.sum(iq16.astype(F32) * ikn, axis=-1, keepdims=True)
        sn = jnp.sum(iwb_ref[0] * jnp.maximum(t, 0.0), axis=0, keepdims=True)
        scnew_ref[0] = jnp.where(sn == 0.0, 0.0, sn)


def _page_specs(shape, group, layer, pool, npg):
    def make(g):
        def index(n, j, pt, *_):
            return (layer * pool + pt[n * npg + j * group + g], 0, 0)
        return pl.BlockSpec(shape, index)
    return [make(g) for g in range(group)]


def _idx_sample(pt_flat, iq3, iwb, kidx_pages, iknew, layer, pool, ns, npg):
    group = _page_group(npg)
    nstep = npg // group
    per = lambda n, j, pt: (n, 0, 0)
    return pl.pallas_call(
        functools.partial(_idx_sample_kernel, group=group),
        grid_spec=pltpu.PrefetchScalarGridSpec(
            num_scalar_prefetch=1, grid=(ns, nstep),
            in_specs=[pl.BlockSpec((1, 16, D_IDX), per), pl.BlockSpec((1, 16, LANE), per)]
                     + _page_specs((1, PAGE_SIZE, D_IDX), group, layer, pool, npg)
                     + [pl.BlockSpec((1, 1, D_IDX), per)],
            out_specs=[pl.BlockSpec((1, nstep, group * PAGE_SIZE), per), pl.BlockSpec((1, 1, LANE), per)]),
        out_shape=[jax.ShapeDtypeStruct((ns, nstep, group * PAGE_SIZE), F32),
                   jax.ShapeDtypeStruct((ns, 1, LANE), F32)],
        compiler_params=_cparams(("parallel", "arbitrary")),
    )(pt_flat, iq3, iwb, *([kidx_pages] * group), iknew)


def _select_sample_kernel(sc_ref, bias_ref, keys_ref, *, top_k, pos_bits):
    nkc = keys_ref.shape[0]
    for c in range(nkc):
        keys_ref[c] = _to_key(sc_ref[c])
    _select_bias(keys_ref, bias_ref, nkc, float(top_k), pos_bits)


def _select_sample(scores3, top_k):
    nkc, rows, kc = scores3.shape
    pos_bits = max(1, int(nkc * kc).bit_length())
    full = pl.BlockSpec((nkc, rows, kc), lambda i: (0, 0, 0))
    return pl.pallas_call(
        functools.partial(_select_sample_kernel, top_k=top_k, pos_bits=pos_bits),
        grid=(1,), in_specs=[full], out_specs=full,
        out_shape=jax.ShapeDtypeStruct((nkc, rows, kc), F32),
        scratch_shapes=[pltpu.VMEM((nkc, rows, kc), I32)],
        compiler_params=_cparams(("arbitrary",)),
    )(scores3)


def _paged_attn_kernel(pt_ref, sc_ref, q_ref, *refs, group, diff):
    k_refs, v_refs = refs[:group], refs[group:2 * group]
    (bias_ref, knew_ref, vnew_ref, biasnew_ref, pick_ref, sub_ref, b64_ref, o_ref,
     k16_ref, v16_ref, m_ref, l_ref, acc_ref) = refs[2 * group:]
    j = pl.program_id(1)

    @pl.when(j == 0)
    def _():
        k16_ref[...] = jnp.zeros_like(k16_ref)
        v16_ref[...] = jnp.zeros_like(v16_ref)
        m_ref[...] = jnp.full_like(m_ref, -jnp.inf)
        l_ref[...] = jnp.zeros_like(l_ref)
        acc_ref[...] = jnp.zeros_like(acc_ref)

    q16 = q_ref[0]
    for g in range(group):
        k16_ref[g * PAGE_SIZE:(g + 1) * PAGE_SIZE, :MIXW] = k_refs[g][0].astype(BF16)
        v16_ref[g * PAGE_SIZE:(g + 1) * PAGE_SIZE, :MIXW] = v_refs[g][0].astype(BF16)
    s = _dot_nt(q16, k16_ref[...]) + bias_ref[0, pl.ds(j, 1), :]
    m, l, acc = _online_update((m_ref[...], l_ref[...], acc_ref[...]), s, v16_ref[...])
    m_ref[...], l_ref[...], acc_ref[...] = m, l, acc

    @pl.when(j == pl.num_programs(1) - 1)
    def _():
        rows_new = knew_ref.shape[1]
        k16_ref[:rows_new, :MIXW] = knew_ref[0].astype(BF16)
        v16_ref[:rows_new, :MIXW] = vnew_ref[0].astype(BF16)
        sn = _dot_nt(q16, k16_ref[:PAGE_SIZE, :]) + biasnew_ref[0]
        mf, lf, accf = _online_update((m, l, acc), sn, v16_ref[:PAGE_SIZE, :])
        o = accf * (1.0 / lf)
        if diff:
            lam, post = sc_ref[0], sc_ref[1]
            o1 = jnp.sum(o * pick_ref[0], axis=0, keepdims=True)
            o2 = jnp.sum(o * pick_ref[1], axis=0, keepdims=True)
            od = o1 - lam * o2
            for pr in range(3):
                ch = od[:, pr * LANE:(pr + 1) * LANE]
                o_ref[0, :, pr * LANE:(pr + 1) * LANE] = (
                    ch * _seg_rsqrt(ch, b64_ref, V_DIFF) * sub_ref[...] * post)
        else:
            o_ref[0] = jnp.sum(o * pick_ref[0], axis=0, keepdims=True)


def _paged_attn(pt_flat, scal, qmaps, k_pages, v_pages, bias3, knew, vnew, biasnew, pick, sub2, b64,
                layer, pool, ns, npg, diff):
    nmap = qmaps.shape[1]
    rows_new = knew.shape[1]
    group = _page_group(npg)
    nstep = npg // group
    per = lambda n, j, pt, sc: (n, 0, 0)
    c2 = lambda n, j, pt, sc: (0, 0)
    c3 = lambda n, j, pt, sc: (0, 0, 0)
    pages = _page_specs((1, PAGE_SIZE, MIXW), group, layer, pool, npg)
    return pl.pallas_call(
        functools.partial(_paged_attn_kernel, group=group, diff=diff),
        grid_spec=pltpu.PrefetchScalarGridSpec(
            num_scalar_prefetch=2, grid=(ns, nstep),
            in_specs=[pl.BlockSpec((1, nmap, W3), per)] + pages + pages
                     + [pl.BlockSpec((1, nstep, group * PAGE_SIZE), per),
                        pl.BlockSpec((1, rows_new, MIXW), per), pl.BlockSpec((1, rows_new, MIXW), per),
                        pl.BlockSpec((1, 1, PAGE_SIZE), per),
                        pl.BlockSpec((2, nmap, W3), c3), pl.BlockSpec((1, LANE), c2), pl.BlockSpec((LANE, LANE), c2)],
            out_specs=pl.BlockSpec((1, 1, W3), per),
            scratch_shapes=[pltpu.VMEM((group * PAGE_SIZE, W3), BF16), pltpu.VMEM((group * PAGE_SIZE, W3), BF16),
                            pltpu.VMEM((nmap, 1), F32), pltpu.VMEM((nmap, 1), F32), pltpu.VMEM((nmap, W3), F32)]),
        out_shape=jax.ShapeDtypeStruct((ns, 1, W3), F32),
        compiler_params=_cparams(("parallel", "arbitrary")),
    )(pt_flat, scal, qmaps, *([k_pages] * group), *([v_pages] * group), bias3, knew, vnew, biasnew, pick, sub2, b64)


def _outproj_kernel(x_ref, a_ref, d_ref, c_ref, wa_ref, wd_ref, wc_ref, g_ref, wg_ref, wu_ref,
                    x1_ref, gate_ref, up_ref):
    x1 = (x_ref[...] + _dot(a_ref[...].astype(BF16), wa_ref[...]) + _dot(d_ref[...].astype(BF16), wd_ref[...])
          + _dot(c_ref[...].astype(BF16), wc_ref[...]))
    x1_ref[...] = x1
    h = (x1 * lax.rsqrt(jnp.mean(x1 * x1, axis=-1, keepdims=True) + EPS) * g_ref[...]).astype(BF16)
    gate_ref[...] = _dot(h, wg_ref[...])
    up_ref[...] = _dot(h, wu_ref[...])


def _outproj(x2d, a, d, c, wa, wd_, wc, gffn, wg, wu, tm):
    rows, dm = x2d.shape
    dff = wg.shape[1]
    row = lambda i: (i, 0)
    const = lambda i: (0, 0)
    return pl.pallas_call(
        _outproj_kernel, grid=(rows // tm,),
        in_specs=[pl.BlockSpec((tm, dm), row)] + [pl.BlockSpec((tm, W3), row)] * 3
                 + [pl.BlockSpec((W3, dm), const)] * 3
                 + [pl.BlockSpec((1, dm), const), pl.BlockSpec((dm, dff), const), pl.BlockSpec((dm, dff), const)],
        out_specs=[pl.BlockSpec((tm, dm), row), pl.BlockSpec((tm, dff), row), pl.BlockSpec((tm, dff), row)],
        out_shape=[jax.ShapeDtypeStruct((rows, dm), F32), jax.ShapeDtypeStruct((rows, dff), F32),
                   jax.ShapeDtypeStruct((rows, dff), F32)],
        compiler_params=_cparams(("parallel",)),
    )(x2d, a, d, c, wa, wd_, wc, gffn, wg, wu)


def _ffn_tail(gm2, gm1, g0, u, x1, cw_ref, cb_ref, wd_ref):
    gc = cw_ref[0:1, :] * gm2 + cw_ref[1:2, :] * gm1 + cw_ref[2:3, :] * g0 + cb_ref[...]
    return x1 + _dot((_silu(gc) * u).astype(BF16), wd_ref[...])


def _ffn_prompt_kernel(g_ref, halo_ref, u_ref, x1_ref, cw_ref, cb_ref, wd_ref, o_ref, gs_ref):
    i = pl.program_id(1)
    tm = g_ref.shape[0]
    halo = halo_ref[...]
    gs_ref[0:8, :] = jnp.where(i == 0, jnp.zeros_like(halo), halo)
    gs_ref[8:8 + tm, :] = g_ref[...]
    o_ref[...] = _ffn_tail(gs_ref[6:6 + tm, :], gs_ref[7:7 + tm, :], g_ref[...], u_ref[...], x1_ref[...],
                           cw_ref, cb_ref, wd_ref)


def _ffn_prompt(gate, up, x1, cw, cb, wd, n, tpad, tm):
    dff, dm = wd.shape
    nb = tpad // tm
    row = lambda b, i: (b * nb + i, 0)
    halo = lambda b, i: (jnp.maximum((b * nb + i) * (tm // 8) - 1, 0), 0)
    const = lambda b, i: (0, 0)
    return pl.pallas_call(
        _ffn_prompt_kernel, grid=(n, nb),
        in_specs=[pl.BlockSpec((tm, dff), row), pl.BlockSpec((8, dff), halo), pl.BlockSpec((tm, dff), row),
                  pl.BlockSpec((tm, dm), row), pl.BlockSpec((CONV_W, dff), const), pl.BlockSpec((1, dff), const),
                  pl.BlockSpec((dff, dm), const)],
        out_specs=pl.BlockSpec((tm, dm), row),
        out_shape=jax.ShapeDtypeStruct((n * tpad, dm), F32),
        scratch_shapes=[pltpu.VMEM((tm + 8, dff), F32)],
        compiler_params=_cparams(("parallel", "arbitrary")),
    )(gate, gate, up, x1, cw, cb, wd)


def _ffn_sample_kernel(g_ref, s0_ref, s1_ref, u_ref, x1_ref, cw_ref, cb_ref, wd_ref, o_ref):
    o_ref[...] = _ffn_tail(s0_ref[...], s1_ref[...], g_ref[...], u_ref[...], x1_ref[...], cw_ref, cb_ref, wd_ref)


def _ffn_sample(gate, s0, s1, up, x1, cw, cb, wd):
    rows, dff = gate.shape
    dm = wd.shape[1]
    full = lambda shape: pl.BlockSpec(shape, lambda i: (0, 0))
    return pl.pallas_call(
        _ffn_sample_kernel, grid=(1,),
        in_specs=[full((rows, dff))] * 4 + [full((rows, dm)), full((CONV_W, dff)), full((1, dff)), full((dff, dm))],
        out_specs=full((rows, dm)),
        out_shape=jax.ShapeDtypeStruct((rows, dm), F32),
        compiler_params=_cparams(("arbitrary",)),
    )(gate, s0, s1, up, x1, cw, cb, wd)


def _rope_tables(pos, head_dim, rot_dim, theta):
    half = rot_dim // 2
    inv = jnp.exp(-math.log(theta) * jnp.arange(half, dtype=F32) / half)
    ang = pos[:, None] * inv[None, :]
    cos, sin = jnp.cos(ang), jnp.sin(ang)
    t = pos.shape[0]
    rest = head_dim - rot_dim
    c = jnp.concatenate([cos, cos, jnp.ones((t, rest), F32)], axis=1)
    s1 = jnp.concatenate([-sin, jnp.zeros((t, half + rest), F32)], axis=1)
    s2 = jnp.concatenate([jnp.zeros((t, half), F32), sin, jnp.zeros((t, rest), F32)], axis=1)
    rep = LANE // head_dim
    return [jnp.tile(a, (1, rep)) for a in (c, s1, s2)]


def _all_tables(pos):
    return (_rope_tables(pos, D_RET, D_RET, RET_THETA)
            + _rope_tables(pos, D_DSA, D_DSA // ROT_FRAC, ROPE_THETA)
            + _rope_tables(pos, D_DIFF, D_DIFF // ROT_FRAC, ROPE_THETA))


def _block_ones(width):
    i = jnp.arange(LANE)
    return (i[:, None] // width == i[None, :] // width).astype(BF16)


def _pad_cols(a, width):
    return jnp.pad(a, [(0, 0)] * (a.ndim - 1) + [(0, width - a.shape[-1])])


def _prep_w_in(w_in):
    parts, off = [], 0
    for size, width in zip(PROJ_SIZES, GROUP_WIDTHS):
        parts.append(_pad_cols(w_in[..., off:off + size], width))
        off += size
    return jnp.concatenate(parts, axis=-1).astype(BF16)


def _prep_w_out(w_out):
    a = w_out[:, :H_RET * D_RET]
    d = jnp.pad(w_out[:, H_RET * D_RET:H_RET * D_RET + MIXW], ((0, 0), (0, W3 - MIXW), (0, 0)))
    c = jnp.pad(w_out[:, H_RET * D_RET + MIXW:], ((0, 0), (0, W3 - MIXW), (0, 0)))
    return a.astype(BF16), d.astype(BF16), c.astype(BF16)


def _tile_gain(g):
    return jnp.tile(g, (1, LANE // g.shape[-1]))[:, None, :]


def kernel(x_prompt, x_sample, cache_dsa_k, cache_dsa_v, cache_dsa_kidx, cache_diff_k, cache_diff_v, state_ret, state_conv, page_table, meta, norm_mix_g, w_in, ret_gn_g, dsa_qn_g, dsa_kn_g, diff_qn_g, diff_kn_g, diff_lq1, diff_lk1, diff_lq2, diff_lk2, diff_subln_g, w_out, norm_ffn_g, ffn_w_gate, ffn_w_up, ffn_conv_w, ffn_conv_b, ffn_w_down):
    nb, seq, dm = x_prompt.shape
    ns, ts, _ = x_sample.shape
    assert ts == 1
    depth = w_in.shape[0]
    dff = ffn_w_gate.shape[-1]
    pool = cache_dsa_k.shape[1]
    npg = page_table.shape[1]
    past = npg * PAGE_SIZE
    tp = seq + N_META
    tpad = -(-tp // K_CHUNK) * K_CHUNK
    k_p = min(TOPK_MAX, tp // 4)
    k_s = min(TOPK_MAX, (past + ts) // 4)
    tm_p = ROW_TILE
    tm_s = ns
    assert ns % SAMPLE_TILE == 0 and ns % 8 == 0

    w_in_p = _prep_w_in(w_in)
    wo_a, wo_d, wo_c = _prep_w_out(w_out)
    wg16, wu16, wd16 = ffn_w_gate.astype(BF16), ffn_w_up.astype(BF16), ffn_w_down.astype(BF16)
    gmix = norm_mix_g[:, None, :]
    gffn = norm_ffn_g[:, None, :]
    qn2, kn2 = _tile_gain(dsa_qn_g), _tile_gain(dsa_kn_g)
    fqn4, fkn4 = _tile_gain(diff_qn_g), _tile_gain(diff_kn_g)
    sub2 = _tile_gain(diff_subln_g)
    gn_flat = ret_gn_g[:, None, :]
    gn_heads = ret_gn_g.reshape(depth, H_RET, D_RET)
    b64, b32 = _block_ones(64), _block_ones(32)
    tabs_p = _all_tables(jnp.arange(tpad, dtype=F32))
    tabs_s = _all_tables(jnp.full((ns,), float(past), F32))
    cb = ffn_conv_b[:, None, :]

    lam_all = (jnp.exp(jnp.sum(diff_lq1 * diff_lk1, axis=-1)) - jnp.exp(jnp.sum(diff_lq2 * diff_lk2, axis=-1)))

    lane_head = jnp.arange(W3) // V_DIFF
    rows16 = jnp.arange(16)
    pick_dsa = jnp.stack([(rows16[:, None] == lane_head[None, :]), jnp.zeros((16, W3), bool)]).astype(F32)
    pick_diff = jnp.stack([(rows16[:, None] == 2 * lane_head[None, :]),
                           (rows16[:, None] == 2 * lane_head[None, :] + 1)]).astype(F32)

    kidx_pages = cache_dsa_kidx.reshape(depth * pool, PAGE_SIZE, D_IDX)
    dsak_pages = cache_dsa_k.reshape(depth * pool, PAGE_SIZE, MIXW)
    dsav_pages = cache_dsa_v.reshape(depth * pool, PAGE_SIZE, MIXW)
    difk_pages = cache_diff_k.reshape(depth * pool, PAGE_SIZE, MIXW)
    difv_pages = cache_diff_v.reshape(depth * pool, PAGE_SIZE, MIXW)
    state4 = state_ret.reshape(depth * ns, H_RET, D_RET, D_RET)
    pt_flat = page_table.reshape(-1).astype(I32)

    xp = jnp.concatenate([jnp.broadcast_to(meta.astype(x_prompt.dtype), (nb, N_META, dm)), x_prompt], axis=1)
    xp = jnp.pad(xp, ((0, 0), (0, tpad - tp), (0, 0))).reshape(nb * tpad, dm)
    xs = x_sample.reshape(ns, dm)

    outs_p = [[] for _ in range(7)]
    outs_s = [[] for _ in range(7)]
    for l in range(depth):
        lam_init = 0.8 - 0.6 * math.exp(-0.3 * l)
        scal = jnp.stack([lam_all[l] + lam_init, jnp.asarray(1.0 - lam_init, F32)]).astype(F32)
        gains = (qn2[l], kn2[l], fqn4[l], fkn4[l])

        (ret4, bq16, bk16, bv16, iq16, ik16, iw, cq16, ck16, cv16, bk32, bv32, ik32, ck32, cv32) = _inproj(
            xp, gmix[l], w_in_p[l], tabs_p, gains, (b64, b32), tm_p)
        a_p, s_p = _ret_prompt(ret4, gn_flat[l], b64, nb, tpad, tp)
        d_p = _dsa_prompt(bq16, iq16, iw, ik16, bk16, bv16, nb, tpad, tp, k_p)
        c_p = _diff_prompt(scal, cq16, ck16, cv16, sub2[l], b64, nb, tpad, tp)
        x1, gate, up = _outproj(xp, a_p, d_p, c_p, wo_a[l], wo_d[l], wo_c[l], gffn[l], wg16[l], wu16[l], tm_p)
        xp = _ffn_prompt(gate, up, x1, ffn_conv_w[l], cb[l], wd16[l], nb, tpad, tm_p)

        def rows_p(a):
            return a.reshape(nb, tpad, -1)[:, :tp]

        outs_p[0].append(s_p)
        outs_p[1].append(gate.reshape(nb, tpad, dff)[:, tp - (CONV_W - 1):tp])
        outs_p[2].append(rows_p(bk32).reshape(nb, tp, H_DSA, D_DSA))
        outs_p[3].append(rows_p(bv32).reshape(nb, tp, H_DSA, D_DSA))
        outs_p[4].append(rows_p(ik32))
        outs_p[5].append(rows_p(ck32).reshape(nb, tp, H_DIFF, 2 * D_DIFF))
        outs_p[6].append(rows_p(cv32).reshape(nb, tp, H_DIFF, V_DIFF))

        (ret4, bq16, bk16, bv16, iq16, ik16, iw, cq16, ck16, cv16, bk32, bv32, ik32, ck32, cv32) = _inproj(
            xs, gmix[l], w_in_p[l], tabs_s, gains, (b64, b32), tm_s)
        nblk = ns // SAMPLE_TILE
        to_cols = lambda a: a.reshape(nblk, SAMPLE_TILE, W3).transpose(0, 2, 1)
        a3, s_s = _ret_sample(to_cols(ret4[:, O_RQ:O_RQ + W3]), to_cols(ret4[:, O_RK:O_RK + W3]),
                              ret4[:, O_RV:O_RV + W3].reshape(ns, H_RET, D_RET),
                              ret4[:, O_RG:O_RG + W3].reshape(ns, H_RET, D_RET), gn_heads[l], state4, l, ns)
        a_s = a3.reshape(ns, W3)
        iq3 = jnp.pad(iq16.reshape(ns, H_IDX, D_IDX), ((0, 0), (0, 16 - H_IDX), (0, 0)))
        iwb = jnp.pad(jnp.broadcast_to(iw[:, :H_IDX, None], (ns, H_IDX, LANE)), ((0, 0), (0, 16 - H_IDX), (0, 0)))
        sc_pages, sc_new = _idx_sample(pt_flat, iq3, iwb, kidx_pages, ik32[:, None, :], l, pool, ns, npg)
        n_keys = past + ts
        kc_s = K_CHUNK
        nkc_s = -(-n_keys // kc_s)
        scores = jnp.concatenate([sc_pages.reshape(ns, past), sc_new[:, 0, :1],
                                  jnp.full((ns, nkc_s * kc_s - n_keys), NEG, F32)], axis=1)
        bias = _select_sample(scores.reshape(ns, nkc_s, kc_s).transpose(1, 0, 2), k_s)
        bias = bias.transpose(1, 0, 2).reshape(ns, nkc_s * kc_s)
        grp = _page_group(npg)
        bias_pages = bias[:, :past].reshape(ns, npg // grp, grp * PAGE_SIZE)
        neg_tail = jnp.full((ns, PAGE_SIZE - 1), NEG, F32)
        bias_new = jnp.concatenate([bias[:, past:past + 1], neg_tail], axis=1)[:, None, :]
        q_dsa = jnp.where(pick_dsa[0][None] > 0, bq16[:, None, :], 0).astype(BF16)
        pad8 = lambda a: jnp.pad(a[:, None, :], ((0, 0), (0, 15), (0, 0)))
        d_s = _paged_attn(pt_flat, scal, q_dsa, dsak_pages, dsav_pages, bias_pages, pad8(bk32), pad8(bv32), bias_new,
                          pick_dsa, sub2[l], b64, l, pool, ns, npg, False).reshape(ns, W3)
        lane_sub = (jnp.arange(W3) % V_DIFF) // D_DIFF
        pick_q = (rows16[:, None] == (2 * lane_head + lane_sub)[None, :])
        q_dif = jnp.where(pick_q[None], cq16[:, None, :], 0).astype(BF16)
        zero_pages = jnp.zeros_like(bias_pages)
        zero_new = jnp.concatenate([jnp.zeros((ns, 1), F32), neg_tail], axis=1)[:, None, :]
        c_s = _paged_attn(pt_flat, scal, q_dif, difk_pages, difv_pages, zero_pages, pad8(ck32), pad8(cv32), zero_new,
                          pick_diff, sub2[l], b64, l, pool, ns, npg, True).reshape(ns, W3)
        x1, gate, up = _outproj(xs, a_s, d_s, c_s, wo_a[l], wo_d[l], wo_c[l], gffn[l], wg16[l], wu16[l], tm_s)
        xs = _ffn_sample(gate, state_conv[l, :, 0], state_conv[l, :, 1], up, x1, ffn_conv_w[l], cb[l], wd16[l])

        outs_s[0].append(s_s)
        outs_s[1].append(jnp.stack([state_conv[l, :, 1], gate], axis=1))
        outs_s[2].append(bk32.reshape(ns, ts, H_DSA, D_DSA))
        outs_s[3].append(bv32.reshape(ns, ts, H_DSA, D_DSA))
        outs_s[4].append(ik32.reshape(ns, ts, D_IDX))
        outs_s[5].append(ck32.reshape(ns, ts, H_DIFF, 2 * D_DIFF))
        outs_s[6].append(cv32.reshape(ns, ts, H_DIFF, V_DIFF))

    ret_p, conv_p, dsak_p, dsav_p, kidx_p, diffk_p, diffv_p = (jnp.stack(a) for a in outs_p)
    ret_s, conv_s, dsak_s, dsav_s, kidx_s, diffk_s, diffv_s = (jnp.stack(a) for a in outs_s)
    y_prompt = xp.reshape(nb, tpad, dm)[:, N_META:tp]
    y_sample = xs.reshape(ns, ts, dm)
    return (y_prompt, y_sample, ret_p, ret_s, conv_p, conv_s, dsak_p, dsak_s, dsav_p, dsav_s,
            kidx_p, kidx_s, diffk_p, diffk_s, diffv_p, diffv_s)
```

```python
import functools
import math
import struct

import jax
import jax.numpy as jnp
from jax import lax
from jax.experimental import pallas as pl
from jax.experimental.pallas import tpu as pltpu

F32 = jnp.float32
BF16 = jnp.bfloat16
I32 = jnp.int32

N_META = 16
H_RET, D_RET = 6, 64
H_DSA, D_DSA = 5, 64
H_IDX, D_IDX = 8, 64
TOPK_MAX = 256
H_DIFF, D_DIFF, V_DIFF = 5, 32, 64
CONV_W = 3
ROPE_THETA = 500000.0
RET_THETA = 10000.0
ROT_FRAC = 4
PAGE_SIZE = 128
NEG = -1e30
EPS = 1e-6
LOG2E = math.log2(math.e)
PROJ_SIZES = ((H_RET * D_RET,) * 4 + (H_DSA * D_DSA,) * 3 + (H_IDX * D_IDX, D_IDX, H_IDX)
              + (2 * H_DIFF * D_DIFF, 2 * H_DIFF * D_DIFF, H_DIFF * V_DIFF))

LANE = 128
VMEM_LIMIT = 56 * 1024 * 1024

W3 = 3 * LANE
GROUP_WIDTHS = (W3, W3, W3, W3, W3, W3, W3, H_IDX * D_IDX, LANE, LANE, W3, W3, W3)
GROUP_OFFS = tuple(int(sum(GROUP_WIDTHS[:i])) for i in range(len(GROUP_WIDTHS)))
IN_PAD = int(sum(GROUP_WIDTHS))
(O_RQ, O_RK, O_RV, O_RG, O_BQ, O_BK, O_BV, O_IQ, O_IK, O_IW, O_CQ, O_CK, O_CV) = GROUP_OFFS
MIXW = H_DSA * D_DSA

ROW_TILE = 256
Q_TILE = 128
K_CHUNK = 512
RET_CHUNK = 128
SAMPLE_TILE = 16


def _float_key(v):
    b = struct.unpack("<i", struct.pack("<f", v))[0]
    return b if b >= 0 else b ^ 0x7FFFFFFF


NEG_KEY = _float_key(NEG)
INT_MAX = 2 ** 31 - 1
INT_MIN = -2 ** 31


def _cparams(sem):
    return pltpu.CompilerParams(dimension_semantics=sem, vmem_limit_bytes=VMEM_LIMIT)


def _seq_spec(shape, index_map):
    return pl.BlockSpec(shape, index_map, pipeline_mode=pl.Buffered(1))


def _dot(a, b):
    return jnp.dot(a, b, preferred_element_type=F32)


def _dot_nt(a, b):
    return lax.dot_general(a, b, (((1,), (1,)), ((), ())), preferred_element_type=F32)


def _silu(x):
    return x * (1.0 / (1.0 + jnp.exp(-x)))


def _seg_rsqrt(x, seg_ref, width):
    x2 = x * x
    hi = x2.astype(BF16)
    lo = (x2 - hi.astype(F32)).astype(BF16)
    seg = seg_ref[...]
    ss = _dot(hi, seg) + _dot(lo, seg)
    return lax.rsqrt(ss * (1.0 / width) + EPS)


def _rope_chunk(x, c, s1, s2, half):
    return x * c + pltpu.roll(x, LANE - half, 1) * s1 + pltpu.roll(x, half, 1) * s2


def _lane_mask(shape, lo, hi):
    lane = lax.broadcasted_iota(I32, shape, len(shape) - 1)
    return (lane >= lo) & (lane < hi)


def _store_value_variants(ref, pair, v):
    low = _lane_mask(v.shape, 0, LANE // 2)
    ref[:, (2 * pair) * LANE:(2 * pair + 1) * LANE] = jnp.where(low, v, 1.0).astype(BF16)
    ref[:, (2 * pair + 1) * LANE:(2 * pair + 2) * LANE] = jnp.where(low, 1.0, v).astype(BF16)


def _value_cols(head):
    return slice(head * LANE, (head + 1) * LANE)


def _inproj_kernel(x_ref, g_ref, w_ref, rc_ref, rs1_ref, rs2_ref, dc_ref, ds1_ref, ds2_ref,
                   fc_ref, fs1_ref, fs2_ref, qn_ref, kn_ref, fqn_ref, fkn_ref, b64_ref, b32_ref,
                   ret4_ref, bq16_ref, bk16_ref, bv16_ref, iq16_ref, ik16_ref, iw_ref,
                   cq16_ref, ck16_ref, cv16_ref, bk32_ref, bv32_ref, ik32_ref, ck32_ref, cv32_ref):
    x = x_ref[...]
    h = x * lax.rsqrt(jnp.mean(x * x, axis=-1, keepdims=True) + EPS) * g_ref[...]
    hb = h.astype(BF16)

    def proj(off, j):
        return _dot(hb, w_ref[:, off + j * LANE: off + (j + 1) * LANE])

    ret_tab = (rc_ref[...], rs1_ref[...], rs2_ref[...], D_RET // 2)
    dsa_tab = (dc_ref[...], ds1_ref[...], ds2_ref[...], D_DSA // ROT_FRAC // 2)
    dif_tab = (fc_ref[...], fs1_ref[...], fs2_ref[...], D_DIFF // ROT_FRAC // 2)

    def rope(p, tab):
        return _rope_chunk(p, *tab)

    for j in range(3):
        sl = slice(j * LANE, (j + 1) * LANE)
        ret4_ref[:, O_RQ + j * LANE: O_RQ + (j + 1) * LANE] = rope(proj(O_RQ, j), ret_tab)
        ret4_ref[:, O_RK + j * LANE: O_RK + (j + 1) * LANE] = rope(proj(O_RK, j), ret_tab) * (D_RET ** -0.5)
        ret4_ref[:, O_RV + j * LANE: O_RV + (j + 1) * LANE] = proj(O_RV, j)
        ret4_ref[:, O_RG + j * LANE: O_RG + (j + 1) * LANE] = proj(O_RG, j)
        p = proj(O_BQ, j)
        bq = rope(p * _seg_rsqrt(p, b64_ref, D_DSA) * qn_ref[...], dsa_tab) * (D_DSA ** -0.5 * LOG2E)
        bq16_ref[:, sl] = bq.astype(BF16)
        p = proj(O_BK, j)
        bk = rope(p * _seg_rsqrt(p, b64_ref, D_DSA) * kn_ref[...], dsa_tab)
        bk16_ref[:, sl] = bk.astype(BF16)
        bv = proj(O_BV, j)
        _store_value_variants(bv16_ref, j, bv)
        p = proj(O_CQ, j)
        cq = rope(p * _seg_rsqrt(p, b32_ref, D_DIFF) * fqn_ref[...], dif_tab) * (D_DIFF ** -0.5 * LOG2E)
        cq16_ref[:, sl] = cq.astype(BF16)
        p = proj(O_CK, j)
        ck = rope(p * _seg_rsqrt(p, b32_ref, D_DIFF) * fkn_ref[...], dif_tab)
        ck16_ref[:, sl] = ck.astype(BF16)
        cv = proj(O_CV, j)
        _store_value_variants(cv16_ref, j, cv)
        w = LANE if j < 2 else MIXW - 2 * LANE
        bk32_ref[:, j * LANE: j * LANE + w] = bk[:, :w]
        bv32_ref[:, j * LANE: j * LANE + w] = bv[:, :w]
        ck32_ref[:, j * LANE: j * LANE + w] = ck[:, :w]
        cv32_ref[:, j * LANE: j * LANE + w] = cv[:, :w]
    for j in range(H_IDX * D_IDX // LANE):
        iq16_ref[:, j * LANE:(j + 1) * LANE] = rope(proj(O_IQ, j), dsa_tab).astype(BF16)
    ik = rope(proj(O_IK, 0), dsa_tab)
    ik32_ref[...] = ik[:, :D_IDX]
    ik16_ref[...] = jnp.where(_lane_mask(ik.shape, 0, D_IDX), ik, pltpu.roll(ik, D_IDX, 1)).astype(BF16)
    iw_ref[...] = proj(O_IW, 0) * ((H_IDX * D_IDX) ** -0.5)


def _inproj(x2d, gmix, w, tabs, gains, segs, tm):
    rows, dm = x2d.shape
    tab_rows = tabs[0].shape[0]
    nb_tab = tab_rows // tm
    grid = (rows // tm,)
    row = lambda i: (i, 0)
    const = lambda i: (0, 0)
    tab = lambda i: (i % nb_tab, 0)
    in_specs = ([pl.BlockSpec((tm, dm), row), pl.BlockSpec((1, dm), const), pl.BlockSpec((dm, IN_PAD), const)]
                + [pl.BlockSpec((tm, LANE), tab)] * 9
                + [pl.BlockSpec((1, LANE), const)] * 4
                + [pl.BlockSpec((LANE, LANE), const)] * 2)
    widths = [(4 * W3, F32), (W3, BF16), (W3, BF16), (2 * W3, BF16), (H_IDX * D_IDX, BF16), (LANE, BF16), (LANE, F32),
              (W3, BF16), (W3, BF16), (2 * W3, BF16), (MIXW, F32), (MIXW, F32), (D_IDX, F32), (MIXW, F32), (MIXW, F32)]
    out_shape = [jax.ShapeDtypeStruct((rows, wd), dt) for wd, dt in widths]
    out_specs = [pl.BlockSpec((tm, wd), row) for wd, _ in widths]
    return pl.pallas_call(
        _inproj_kernel, grid=grid, in_specs=in_specs, out_specs=out_specs, out_shape=out_shape,
        compiler_params=_cparams(("parallel",)),
    )(x2d, gmix, w, *tabs, *gains, *segs)


def _ret_lg(h):
    return math.log(1.0 - 2.0 ** (-5.0 - h))


def _gate(o, g, gn, b64_ref):
    return _silu(g) * (o * _seg_rsqrt(o, b64_ref, D_RET) * gn)


def _ret_prompt_kernel(ret4_ref, gn_ref, b64_ref, a_ref, sfin_ref, s_ref, *, tp):
    c = pl.program_id(1)
    ch = RET_CHUNK

    @pl.when(c == 0)
    def _():
        s_ref[...] = jnp.zeros_like(s_ref)

    valid = jnp.clip(tp - c * ch, 0, ch).astype(F32)
    ri = lax.broadcasted_iota(I32, (ch, 1), 0).astype(F32)
    rel = (lax.broadcasted_iota(I32, (ch, ch), 0) - lax.broadcasted_iota(I32, (ch, ch), 1)).astype(F32)
    lane = lax.broadcasted_iota(I32, (1, LANE), 1)
    rowi = lax.broadcasted_iota(I32, (LANE, 1), 0)
    for p in range(H_RET // 2):
        sl = slice(p * LANE, (p + 1) * LANE)
        lg_a, lg_b = _ret_lg(2 * p), _ret_lg(2 * p + 1)
        q = ret4_ref[:, O_RQ + p * LANE: O_RQ + (p + 1) * LANE]
        k = ret4_ref[:, O_RK + p * LANE: O_RK + (p + 1) * LANE]
        v = ret4_ref[:, O_RV + p * LANE: O_RV + (p + 1) * LANE]
        k16 = k.astype(BF16)
        v16 = v.astype(BF16)
        s_pair = s_ref[p]
        s16 = s_pair.astype(BF16)
        outs = []
        for half, lg in ((0, lg_a), (1, lg_b)):
            qm = jnp.where(_lane_mask(q.shape, half * D_RET, (half + 1) * D_RET), q, 0.0).astype(BF16)
            dmat = jnp.where(rel >= 0.0, jnp.exp(jnp.maximum(rel, 0.0) * lg), 0.0)
            inner = _dot_nt(qm, k16) * dmat
            outs.append(_dot(inner.astype(BF16), v16) + _dot(qm, s16) * jnp.exp((ri + 1.0) * lg))
        o_pair = jnp.where(lane < D_RET, outs[0], outs[1])
        lgv = jnp.where(lane < D_RET, lg_a, lg_b)
        kw = k * jnp.where(ri < valid, jnp.exp((valid - 1.0 - ri) * lgv), 0.0)
        lgc = jnp.where(rowi < D_RET, lg_a, lg_b)
        s_new = jnp.exp(valid * lgc) * s_pair + _dot(kw.T.astype(BF16), v16)
        s_ref[p] = s_new
        g = ret4_ref[:, O_RG + p * LANE: O_RG + (p + 1) * LANE]
        a_ref[:, sl] = _gate(o_pair, g, gn_ref[:, sl], b64_ref)

    @pl.when(c == pl.num_programs(1) - 1)
    def _():
        for h in range(H_RET):
            blk = s_ref[h // 2]
            o = (h % 2) * D_RET
            sfin_ref[0, h] = blk[o:o + D_RET, o:o + D_RET]


def _ret_prompt(ret4, gn, b64, n, tpad, tp):
    nc = tpad // RET_CHUNK
    return pl.pallas_call(
        functools.partial(_ret_prompt_kernel, tp=tp),
        grid=(n, nc),
        in_specs=[pl.BlockSpec((RET_CHUNK, 4 * W3), lambda b, c: (b * nc + c, 0)),
                  pl.BlockSpec((1, W3), lambda b, c: (0, 0)),
                  pl.BlockSpec((LANE, LANE), lambda b, c: (0, 0))],
        out_specs=[pl.BlockSpec((RET_CHUNK, W3), lambda b, c: (b * nc + c, 0)),
                   pl.BlockSpec((1, H_RET, D_RET, D_RET), lambda b, c: (b, 0, 0, 0))],
        out_shape=[jax.ShapeDtypeStruct((n * tpad, W3), F32),
                   jax.ShapeDtypeStruct((n, H_RET, D_RET, D_RET), F32)],
        scratch_shapes=[pltpu.VMEM((H_RET // 2, LANE, LANE), F32)],
        compiler_params=_cparams(("parallel", "arbitrary")),
    )(ret4, gn, b64)


def _ret_sample_kernel(qt_ref, kt_ref, v_ref, g_ref, gn_ref, st_ref, a_ref, snew_ref, o_ref):
    nb = v_ref.shape[0]
    for j in range(nb):
        for h in range(H_RET):
            gamma = math.exp(_ret_lg(h))
            qcol = qt_ref[0, h * D_RET:(h + 1) * D_RET, j:j + 1]
            kcol = kt_ref[0, h * D_RET:(h + 1) * D_RET, j:j + 1]
            vrow = v_ref[j, h:h + 1, :]
            s_new = gamma * st_ref[j, h] + kcol * vrow
            snew_ref[j, h] = s_new
            o_ref[j, h:h + 1, :] = jnp.sum(qcol * s_new, axis=0, keepdims=True)
    o = o_ref[...]
    r = lax.rsqrt(jnp.mean(o * o, axis=-1, keepdims=True) + EPS)
    a_ref[...] = _silu(g_ref[...]) * (o * r * gn_ref[...])


def _ret_sample(qt, kt, v3, g3, gn2, state4, layer, ns):
    nb = qt.shape[2]
    nblk = ns // nb
    blk3 = pl.BlockSpec((nb, H_RET, D_RET), lambda i: (i, 0, 0))
    tsp = pl.BlockSpec((1, H_RET * D_RET, nb), lambda i: (i, 0, 0))
    return pl.pallas_call(
        _ret_sample_kernel,
        grid=(nblk,),
        in_specs=[tsp, tsp, blk3, blk3,
                  pl.BlockSpec((H_RET, D_RET), lambda i: (0, 0)),
                  pl.BlockSpec((nb, H_RET, D_RET, D_RET), lambda i: (layer * nblk + i, 0, 0, 0))],
        out_specs=[blk3, pl.BlockSpec((nb, H_RET, D_RET, D_RET), lambda i: (i, 0, 0, 0))],
        out_shape=[jax.ShapeDtypeStruct((ns, H_RET, D_RET), F32),
                   jax.ShapeDtypeStruct((ns, H_RET, D_RET, D_RET), F32)],
        scratch_shapes=[pltpu.VMEM((nb, H_RET, D_RET), F32)],
        compiler_params=_cparams(("parallel",)),
    )(qt, kt, v3, g3, gn2, state4)


def _to_key(s):
    b = lax.bitcast_convert_type(s, I32)
    return jnp.where(b >= 0, b, b ^ jnp.int32(0x7FFFFFFF))


def _fold_lanes(x):
    acc = x[:, :LANE]
    for t in range(1, x.shape[1] // LANE):
        acc = acc + x[:, t * LANE:(t + 1) * LANE]
    return acc


def _select_bias(keys_ref, bias_ref, nkc, k, pos_bits):
    _, rows, kc = keys_ref.shape
    groups = kc // LANE
    lane_pos = lax.broadcasted_iota(I32, (rows, LANE), 1)
    wide = lambda col: jnp.broadcast_to(col, (rows, LANE))

    def count(ones):
        def body(c, acc):
            for g in range(groups):
                acc = acc + ones(keys_ref[c, :, g * LANE:(g + 1) * LANE], c * kc + g * LANE + lane_pos)
            return acc
        acc = lax.fori_loop(0, nkc, body, jnp.zeros((rows, LANE), F32))
        return jnp.sum(acc, axis=-1, keepdims=True)

    def thr_step(it, thr):
        cand = thr + lax.shift_left(jnp.int32(1), 31 - it)
        cand_b = wide(cand)
        tot = count(lambda kk, kpos: jnp.where(kk >= cand_b, 1.0, 0.0))
        return jnp.where(tot >= k, cand, thr)

    thr = lax.fori_loop(0, 32, thr_step, jnp.full((rows, 1), INT_MIN, I32))
    thr_b = wide(thr)
    c_gt = count(lambda kk, kpos: jnp.where(kk > thr_b, 1.0, 0.0))
    c_ge = count(lambda kk, kpos: jnp.where(kk >= thr_b, 1.0, 0.0))
    need = k - c_gt
    excess = jnp.where(thr > NEG_KEY, c_ge - k, 0.0)

    def tie_search():
        def step(it, lo):
            cand = lo + lax.shift_left(jnp.int32(1), pos_bits - 1 - it)
            cand_b = wide(cand)
            f = count(lambda kk, kpos: jnp.where(kk == thr_b, jnp.where(kpos < cand_b, 1.0, 0.0), 0.0))
            return jnp.where(f < need, cand, lo)
        return lax.fori_loop(0, pos_bits, step, jnp.zeros((rows, 1), I32))

    last = lax.cond(jnp.max(excess) > 0.0, tie_search, lambda: jnp.full((rows, 1), INT_MAX, I32))
    last_b = wide(last)

    def emit(c, carry):
        for g in range(groups):
            cols = slice(g * LANE, (g + 1) * LANE)
            kk = keys_ref[c, :, cols]
            tie = jnp.where(c * kc + g * LANE + lane_pos <= last_b, 0.0, NEG)
            sel = jnp.where(kk > thr_b, 0.0, jnp.where(kk == thr_b, tie, NEG))
            bias_ref[c, :, cols] = jnp.where(kk > NEG_KEY, sel, NEG)
        return carry

    lax.fori_loop(0, nkc, emit, 0)


def _fold_max(x):
    acc = x[:, :LANE]
    for t in range(1, x.shape[1] // LANE):
        acc = jnp.maximum(acc, x[:, t * LANE:(t + 1) * LANE])
    return acc


def _stacked_scores(q_stack, sizes, k16):
    s = _dot_nt(q_stack, k16)
    rows = s.shape[0] // sizes
    return [s[i * rows:(i + 1) * rows] for i in range(sizes)]


def _softmax_pv(s_ref, n_chunks, maccs, v_groups):
    nmap, _, rows, kc = s_ref.shape
    mbs = [jnp.broadcast_to(jnp.max(m, axis=-1, keepdims=True), (rows, LANE)) for m in maccs]

    def probs(i, c):
        ps = [jnp.exp2(s_ref[i, c, :, g * LANE:(g + 1) * LANE] - mbs[i]) for g in range(kc // LANE)]
        return jnp.concatenate(ps, axis=1).astype(BF16)

    def pass_pv(c, accs):
        new = list(accs)
        for maps, v_chunk in v_groups:
            pv = _dot(jnp.concatenate([probs(i, c) for i in maps], axis=0), v_chunk(c))
            for t, i in enumerate(maps):
                new[i] = accs[i] + pv[t * rows:(t + 1) * rows]
        return tuple(new)

    accs = lax.fori_loop(0, n_chunks, pass_pv, tuple(jnp.zeros((rows, LANE), F32) for _ in range(nmap)))
    return [acc * (1.0 / pltpu.roll(acc, LANE // 2, 1)) for acc in accs]


def _online_update(carry, s, v16):
    m, l, acc = carry
    m_new = jnp.maximum(m, jnp.max(s, axis=-1, keepdims=True))
    alpha = jnp.exp2(m - m_new)
    p = jnp.exp2(s - m_new)
    l = alpha * l + jnp.sum(p, axis=-1, keepdims=True)
    acc = alpha * acc + _dot(p.astype(BF16), v16)
    return m_new, l, acc


def _online_init(rows, width):
    return (jnp.full((rows, 1), -jnp.inf, F32), jnp.zeros((rows, 1), F32), jnp.zeros((rows, width), F32))


def _dsa_prompt_kernel(bq_ref, iq_ref, iw_ref, ik_ref, bk_ref, bv_ref, o_ref, keys_ref, bias_ref, s_ref,
                       *, top_k, pos_bits, tp):
    q0 = pl.program_id(1) * Q_TILE

    @pl.when(q0 < tp)
    def _():
        _dsa_prompt_tile(bq_ref, iq_ref, iw_ref, ik_ref, bk_ref, bv_ref, o_ref, keys_ref, bias_ref, s_ref,
                         q0, top_k, pos_bits)

    @pl.when(q0 >= tp)
    def _():
        o_ref[...] = jnp.zeros_like(o_ref)


def _dsa_prompt_tile(bq_ref, iq_ref, iw_ref, ik_ref, bk_ref, bv_ref, o_ref, keys_ref, bias_ref, s_ref,
                     q0, top_k, pos_bits):
    tq, kc = Q_TILE, K_CHUNK
    nkc = (q0 + tq + kc - 1) // kc
    qpos = q0 + lax.broadcasted_iota(I32, (tq, 1), 0)
    iw = iw_ref[...]

    iq_heads = []
    for h in range(H_IDX):
        ch = iq_ref[:, (h // 2) * LANE:(h // 2 + 1) * LANE]
        iq_heads.append(jnp.where(_lane_mask(ch.shape, (h % 2) * D_IDX, (h % 2 + 1) * D_IDX), ch, 0).astype(BF16))

    def score_body(c, carry):
        kch = ik_ref[pl.ds(pl.multiple_of(c * kc, kc), kc), :]
        acc = jnp.zeros((tq, kc), F32)
        for h in range(H_IDX):
            acc = acc + iw[:, h:h + 1] * jnp.maximum(_dot_nt(iq_heads[h], kch), 0.0)
        acc = jnp.where(acc == 0.0, 0.0, acc)
        kpos = c * kc + lax.broadcasted_iota(I32, (1, kc), 1)
        keys_ref[c] = _to_key(jnp.where(kpos <= qpos, acc, NEG))
        return carry

    lax.fori_loop(0, nkc, score_body, 0)
    _select_bias(keys_ref, bias_ref, nkc, float(top_k), pos_bits)

    lane = lax.broadcasted_iota(I32, (1, LANE), 1)
    qms = []
    for h in range(H_DSA):
        ch = bq_ref[:, (h // 2) * LANE:(h // 2 + 1) * LANE]
        qms.append(jnp.where(_lane_mask(ch.shape, (h % 2) * D_DSA, (h % 2 + 1) * D_DSA), ch, 0).astype(BF16))

    pairs = [list(range(2 * p, min(2 * p + 2, H_DSA))) for p in range((H_DSA + 1) // 2)]
    q_stacks = [jnp.concatenate([qms[h] for h in hs], axis=0) for hs in pairs]

    def pass_qk(c, maccs):
        rows = pl.ds(pl.multiple_of(c * kc, kc), kc)
        bias = bias_ref[c]
        new = list(maccs)
        for p, hs in enumerate(pairs):
            blocks = _stacked_scores(q_stacks[p], len(hs), bk_ref[rows, p * LANE:(p + 1) * LANE])
            for h, s in zip(hs, blocks):
                s = s + bias
                s_ref[h, c] = s
                new[h] = jnp.maximum(maccs[h], _fold_max(s))
        return tuple(new)

    maccs = lax.fori_loop(0, nkc, pass_qk, tuple(jnp.full((tq, LANE), -jnp.inf, F32) for _ in range(H_DSA)))
    v_groups = [([h], lambda c, h=h: bv_ref[pl.ds(pl.multiple_of(c * kc, kc), kc), _value_cols(h)])
                for h in range(H_DSA)]
    outs = _softmax_pv(s_ref, nkc, maccs, v_groups)
    outs.append(jnp.zeros_like(outs[0]))
    for pr in range(3):
        o_ref[:, pr * LANE:(pr + 1) * LANE] = jnp.where(lane < D_DSA, outs[2 * pr], outs[2 * pr + 1])


def _dsa_prompt(bq16, iq16, iw, ik16, bk16, bv16, n, tpad, tp, top_k):
    nq = tpad // Q_TILE
    nkc = tpad // K_CHUNK
    pos_bits = max(1, int(tpad).bit_length())
    qrow = lambda b, i: (b * nq + i, 0)
    seq = lambda b, i: (b, 0)
    return pl.pallas_call(
        functools.partial(_dsa_prompt_kernel, top_k=top_k, pos_bits=pos_bits, tp=tp),
        grid=(n, nq),
        in_specs=[pl.BlockSpec((Q_TILE, W3), qrow), pl.BlockSpec((Q_TILE, H_IDX * D_IDX), qrow),
                  pl.BlockSpec((Q_TILE, LANE), qrow),
                  _seq_spec((tpad, LANE), seq), _seq_spec((tpad, W3), seq), _seq_spec((tpad, 2 * W3), seq)],
        out_specs=pl.BlockSpec((Q_TILE, W3), qrow),
        out_shape=jax.ShapeDtypeStruct((n * tpad, W3), F32),
        scratch_shapes=[pltpu.VMEM((nkc, Q_TILE, K_CHUNK), I32), pltpu.VMEM((nkc, Q_TILE, K_CHUNK), F32),
                        pltpu.VMEM((H_DSA, nkc, Q_TILE, K_CHUNK), F32)],
        compiler_params=_cparams(("parallel", "arbitrary")),
    )(bq16, iq16, iw, ik16, bk16, bv16)


def _diff_finish(o, post, sub, b64_ref):
    return o * _seg_rsqrt(o, b64_ref, V_DIFF) * sub * post


def _diff_prompt_kernel(sc_ref, cq_ref, ck_ref, cv_ref, sub_ref, b64_ref, o_ref, s_ref, *, tp):
    q0 = pl.program_id(1) * Q_TILE

    @pl.when(q0 < tp)
    def _():
        _diff_prompt_tile(sc_ref, cq_ref, ck_ref, cv_ref, sub_ref, b64_ref, o_ref, s_ref, q0)

    @pl.when(q0 >= tp)
    def _():
        o_ref[...] = jnp.zeros_like(o_ref)


def _diff_prompt_tile(sc_ref, cq_ref, ck_ref, cv_ref, sub_ref, b64_ref, o_ref, s_ref, q0):
    tq, kc = Q_TILE, K_CHUNK
    nfull = q0 // kc
    qpos = q0 + lax.broadcasted_iota(I32, (tq, 1), 0)
    lam, post = sc_ref[0], sc_ref[1]
    lane = lax.broadcasted_iota(I32, (1, LANE), 1)
    qmaps = []
    for r in range(2 * H_DIFF):
        ch = cq_ref[:, (r // 4) * LANE:(r // 4 + 1) * LANE]
        base = (r % 4) * D_DIFF
        qmaps.append(jnp.where(_lane_mask(ch.shape, base, base + D_DIFF), ch, 0).astype(BF16))

    ok = (nfull * kc + lax.broadcasted_iota(I32, (1, kc), 1)) <= qpos
    fin = []
    for grp in range(2):
        maps = list(range(grp * H_DIFF, (grp + 1) * H_DIFF))
        by_chunk = [[i for i, r in enumerate(maps) if r // 4 == ch] for ch in range(3)]
        by_chunk = [(ch, idx) for ch, idx in enumerate(by_chunk) if idx]
        q_stacks = [jnp.concatenate([qmaps[maps[i]] for i in idx], axis=0) for _, idx in by_chunk]
        by_head = [[i for i, r in enumerate(maps) if r // 2 == h] for h in range(H_DIFF)]
        v_groups = [(idx, lambda c, h=h: cv_ref[pl.ds(pl.multiple_of(c * kc, kc), kc), _value_cols(h)])
                    for h, idx in enumerate(by_head) if idx]

        def pass_qk(c, maccs, by_chunk=by_chunk, q_stacks=q_stacks, masked=False):
            rows = pl.ds(pl.multiple_of(c * kc, kc), kc)
            new = list(maccs)
            for (ch, idx), q_stack in zip(by_chunk, q_stacks):
                blocks = _stacked_scores(q_stack, len(idx), ck_ref[rows, ch * LANE:(ch + 1) * LANE])
                for i, s in zip(idx, blocks):
                    if masked:
                        s = jnp.where(ok, s, NEG)
                    s_ref[i, c] = s
                    new[i] = jnp.maximum(maccs[i], _fold_max(s))
            return tuple(new)

        maccs = lax.fori_loop(0, nfull, pass_qk, tuple(jnp.full((tq, LANE), -jnp.inf, F32) for _ in maps))
        maccs = pass_qk(nfull, maccs, masked=True)
        fin.extend(_softmax_pv(s_ref, nfull + 1, maccs, v_groups))
    heads = [fin[2 * h] - lam * fin[2 * h + 1] for h in range(H_DIFF)]
    heads.append(jnp.zeros_like(heads[0]))
    for pr in range(3):
        o = jnp.where(lane < V_DIFF, heads[2 * pr], heads[2 * pr + 1])
        o_ref[:, pr * LANE:(pr + 1) * LANE] = _diff_finish(o, post, sub_ref[...], b64_ref)


def _diff_prompt(scal, cq16, ck16, cv16, sub2, b64, n, tpad, tp):
    nq = tpad // Q_TILE
    qrow = lambda b, i, sc: (b * nq + i, 0)
    seq = lambda b, i, sc: (b, 0)
    const = lambda b, i, sc: (0, 0)
    return pl.pallas_call(
        functools.partial(_diff_prompt_kernel, tp=tp),
        grid_spec=pltpu.PrefetchScalarGridSpec(
            num_scalar_prefetch=1, grid=(n, nq),
            in_specs=[pl.BlockSpec((Q_TILE, W3), qrow), _seq_spec((tpad, W3), seq), _seq_spec((tpad, 2 * W3), seq),
                      pl.BlockSpec((1, LANE), const), pl.BlockSpec((LANE, LANE), const)],
            out_specs=pl.BlockSpec((Q_TILE, W3), qrow),
            scratch_shapes=[pltpu.VMEM((H_DIFF, tpad // K_CHUNK, Q_TILE, K_CHUNK), F32)]),
        out_shape=jax.ShapeDtypeStruct((n * tpad, W3), F32),
        compiler_params=_cparams(("parallel", "arbitrary")),
    )(scal, cq16, ck16, cv16, sub2, b64)


def _page_group(npg):
    return next(g for g in (8, 4, 2, 1) if npg % g == 0)


def _idx_sample_kernel(pt_ref, iq_ref, iwb_ref, *refs, group):
    kidx_refs = refs[:group]
    iknew_ref, sc_ref, scnew_ref = refs[group:]
    j = pl.program_id(1)
    iq16 = iq_ref[0]
    iw_col = iwb_ref[0][:, 0:1]
    rows = []
    for g in range(group):
        s = _dot_nt(iq16, kidx_refs[g][0].astype(BF16))
        rows.append(jnp.sum(iw_col * jnp.maximum(s, 0.0), axis=0, keepdims=True))
    row = jnp.concatenate(rows, axis=1)
    sc_ref[0, pl.ds(j, 1), :] = jnp.where(row == 0.0, 0.0, row)

    @pl.when(j == pl.num_programs(1) - 1)
    def _():
        ikn = iknew_ref[0].astype(BF16).astype(F32)
        t = jnp.sum(iq16.astype(F32) * ikn, axis=-1, keepdims=True)
        sn = jnp.sum(iwb_ref[0] * jnp.maximum(t, 0.0), axis=0, keepdims=True)
        scnew_ref[0] = jnp.where(sn == 0.0, 0.0, sn)


def _page_specs(shape, group, layer, pool, npg):
    def make(g):
        def index(n, j, pt, *_):
            return (layer * pool + pt[n * npg + j * group + g], 0, 0)
        return pl.BlockSpec(shape, index)
    return [make(g) for g in range(group)]


def _idx_sample(pt_flat, iq3, iwb, kidx_pages, iknew, layer, pool, ns, npg):
    group = _page_group(npg)
    nstep = npg // group
    per = lambda n, j, pt: (n, 0, 0)
    return pl.pallas_call(
        functools.partial(_idx_sample_kernel, group=group),
        grid_spec=pltpu.PrefetchScalarGridSpec(
            num_scalar_prefetch=1, grid=(ns, nstep),
            in_specs=[pl.BlockSpec((1, 16, D_IDX), per), pl.BlockSpec((1, 16, LANE), per)]
                     + _page_specs((1, PAGE_SIZE, D_IDX), group, layer, pool, npg)
                     + [pl.BlockSpec((1, 1, D_IDX), per)],
            out_specs=[pl.BlockSpec((1, nstep, group * PAGE_SIZE), per), pl.BlockSpec((1, 1, LANE), per)]),
        out_shape=[jax.ShapeDtypeStruct((ns, nstep, group * PAGE_SIZE), F32),
                   jax.ShapeDtypeStruct((ns, 1, LANE), F32)],
        compiler_params=_cparams(("parallel", "arbitrary")),
    )(pt_flat, iq3, iwb, *([kidx_pages] * group), iknew)


def _select_sample_kernel(sc_ref, bias_ref, keys_ref, *, top_k, pos_bits):
    nkc = keys_ref.shape[0]
    for c in range(nkc):
        keys_ref[c] = _to_key(sc_ref[c])
    _select_bias(keys_ref, bias_ref, nkc, float(top_k), pos_bits)


def _select_sample(scores3, top_k):
    nkc, rows, kc = scores3.shape
    pos_bits = max(1, int(nkc * kc).bit_length())
    full = pl.BlockSpec((nkc, rows, kc), lambda i: (0, 0, 0))
    return pl.pallas_call(
        functools.partial(_select_sample_kernel, top_k=top_k, pos_bits=pos_bits),
        grid=(1,), in_specs=[full], out_specs=full,
        out_shape=jax.ShapeDtypeStruct((nkc, rows, kc), F32),
        scratch_shapes=[pltpu.VMEM((nkc, rows, kc), I32)],
        compiler_params=_cparams(("arbitrary",)),
    )(scores3)


def _paged_attn_kernel(pt_ref, sc_ref, q_ref, *refs, group, diff):
    k_refs, v_refs = refs[:group], refs[group:2 * group]
    (bias_ref, knew_ref, vnew_ref, biasnew_ref, pick_ref, sub_ref, b64_ref, o_ref,
     m_ref, l_ref, acc_ref) = refs[2 * group:]
    j = pl.program_id(1)

    @pl.when(j == 0)
    def _():
        m_ref[...] = jnp.full_like(m_ref, -jnp.inf)
        l_ref[...] = jnp.zeros_like(l_ref)
        acc_ref[...] = jnp.zeros_like(acc_ref)

    q16 = q_ref[0]
    s = jnp.concatenate([_dot_nt(q16, k_refs[g][0]) for g in range(group)], axis=1) + bias_ref[0, pl.ds(j, 1), :]
    m_old = m_ref[...]
    m = jnp.maximum(m_old, jnp.max(s, axis=-1, keepdims=True))
    alpha = jnp.exp2(m_old - m)
    p = jnp.exp2(s - m)
    l = alpha * l_ref[...] + jnp.sum(p, axis=-1, keepdims=True)
    p16 = p.astype(BF16)
    acc = alpha * acc_ref[...]
    for g in range(group):
        acc = acc + _dot(p16[:, g * PAGE_SIZE:(g + 1) * PAGE_SIZE], v_refs[g][0])
    m_ref[...], l_ref[...], acc_ref[...] = m, l, acc

    @pl.when(j == pl.num_programs(1) - 1)
    def _():
        sn = jnp.sum(q16.astype(F32) * knew_ref[0], axis=-1, keepdims=True) + biasnew_ref[0][:, 0:1]
        mf = jnp.maximum(m, sn)
        af = jnp.exp2(m - mf)
        pn = jnp.exp2(sn - mf)
        lf = af * l + pn
        accf = af * acc + pn * vnew_ref[0]
        o = accf * (1.0 / lf)
        if diff:
            lam, post = sc_ref[0], sc_ref[1]
            o1 = jnp.sum(o * pick_ref[0], axis=0, keepdims=True)
            o2 = jnp.sum(o * pick_ref[1], axis=0, keepdims=True)
            od = o1 - lam * o2
            for pr in range(3):
                ch = od[:, pr * LANE:(pr + 1) * LANE]
                o_ref[0, :, pr * LANE:(pr + 1) * LANE] = (
                    ch * _seg_rsqrt(ch, b64_ref, V_DIFF) * sub_ref[...] * post)
        else:
            o_ref[0] = jnp.sum(o * pick_ref[0], axis=0, keepdims=True)


def _paged_attn(pt_flat, scal, qmaps, k_pages, v_pages, bias3, knew, vnew, biasnew, pick, sub2, b64,
                layer, pool, ns, npg, diff):
    nmap = qmaps.shape[1]
    group = _page_group(npg)
    nstep = npg // group
    per = lambda n, j, pt, sc: (n, 0, 0)
    c2 = lambda n, j, pt, sc: (0, 0)
    c3 = lambda n, j, pt, sc: (0, 0, 0)
    pages = _page_specs((1, PAGE_SIZE, W3), group, layer, pool, npg)
    return pl.pallas_call(
        functools.partial(_paged_attn_kernel, group=group, diff=diff),
        grid_spec=pltpu.PrefetchScalarGridSpec(
            num_scalar_prefetch=2, grid=(ns, nstep),
            in_specs=[pl.BlockSpec((1, nmap, W3), per)] + pages + pages
                     + [pl.BlockSpec((1, nstep, group * PAGE_SIZE), per),
                        pl.BlockSpec((1, 1, W3), per), pl.BlockSpec((1, 1, W3), per),
                        pl.BlockSpec((1, 1, LANE), per),
                        pl.BlockSpec((2, nmap, W3), c3), pl.BlockSpec((1, LANE), c2), pl.BlockSpec((LANE, LANE), c2)],
            out_specs=pl.BlockSpec((1, 1, W3), per),
            scratch_shapes=[pltpu.VMEM((nmap, 1), F32), pltpu.VMEM((nmap, 1), F32), pltpu.VMEM((nmap, W3), F32)]),
        out_shape=jax.ShapeDtypeStruct((ns, 1, W3), F32),
        compiler_params=_cparams(("parallel", "arbitrary")),
    )(pt_flat, scal, qmaps, *([k_pages] * group), *([v_pages] * group), bias3, knew, vnew, biasnew, pick, sub2, b64)


def _outproj_kernel(x_ref, a_ref, d_ref, c_ref, wa_ref, wd_ref, wc_ref, g_ref, wg_ref, wu_ref,
                    x1_ref, gate_ref, up_ref):
    x1 = (x_ref[...] + _dot(a_ref[...].astype(BF16), wa_ref[...]) + _dot(d_ref[...].astype(BF16), wd_ref[...])
          + _dot(c_ref[...].astype(BF16), wc_ref[...]))
    x1_ref[...] = x1
    h = (x1 * lax.rsqrt(jnp.mean(x1 * x1, axis=-1, keepdims=True) + EPS) * g_ref[...]).astype(BF16)
    gate_ref[...] = _dot(h, wg_ref[...])
    up_ref[...] = _dot(h, wu_ref[...])


def _outproj(x2d, a, d, c, wa, wd_, wc, gffn, wg, wu, tm):
    rows, dm = x2d.shape
    dff = wg.shape[1]
    row = lambda i: (i, 0)
    const = lambda i: (0, 0)
    return pl.pallas_call(
        _outproj_kernel, grid=(rows // tm,),
        in_specs=[pl.BlockSpec((tm, dm), row)] + [pl.BlockSpec((tm, W3), row)] * 3
                 + [pl.BlockSpec((W3, dm), const)] * 3
                 + [pl.BlockSpec((1, dm), const), pl.BlockSpec((dm, dff), const), pl.BlockSpec((dm, dff), const)],
        out_specs=[pl.BlockSpec((tm, dm), row), pl.BlockSpec((tm, dff), row), pl.BlockSpec((tm, dff), row)],
        out_shape=[jax.ShapeDtypeStruct((rows, dm), F32), jax.ShapeDtypeStruct((rows, dff), F32),
                   jax.ShapeDtypeStruct((rows, dff), F32)],
        compiler_params=_cparams(("parallel",)),
    )(x2d, a, d, c, wa, wd_, wc, gffn, wg, wu)


def _ffn_tail(gm2, gm1, g0, u, x1, cw_ref, cb_ref, wd_ref):
    gc = cw_ref[0:1, :] * gm2 + cw_ref[1:2, :] * gm1 + cw_ref[2:3, :] * g0 + cb_ref[...]
    return x1 + _dot((_silu(gc) * u).astype(BF16), wd_ref[...])


def _ffn_prompt_kernel(g_ref, halo_ref, u_ref, x1_ref, cw_ref, cb_ref, wd_ref, o_ref, gs_ref):
    i = pl.program_id(1)
    tm = g_ref.shape[0]
    halo = halo_ref[...]
    gs_ref[0:8, :] = jnp.where(i == 0, jnp.zeros_like(halo), halo)
    gs_ref[8:8 + tm, :] = g_ref[...]
    o_ref[...] = _ffn_tail(gs_ref[6:6 + tm, :], gs_ref[7:7 + tm, :], g_ref[...], u_ref[...], x1_ref[...],
                           cw_ref, cb_ref, wd_ref)


def _ffn_prompt(gate, up, x1, cw, cb, wd, n, tpad, tm):
    dff, dm = wd.shape
    nb = tpad // tm
    row = lambda b, i: (b * nb + i, 0)
    halo = lambda b, i: (jnp.maximum((b * nb + i) * (tm // 8) - 1, 0), 0)
    const = lambda b, i: (0, 0)
    return pl.pallas_call(
        _ffn_prompt_kernel, grid=(n, nb),
        in_specs=[pl.BlockSpec((tm, dff), row), pl.BlockSpec((8, dff), halo), pl.BlockSpec((tm, dff), row),
                  pl.BlockSpec((tm, dm), row), pl.BlockSpec((CONV_W, dff), const), pl.BlockSpec((1, dff), const),
                  pl.BlockSpec((dff, dm), const)],
        out_specs=pl.BlockSpec((tm, dm), row),
        out_shape=jax.ShapeDtypeStruct((n * tpad, dm), F32),
        scratch_shapes=[pltpu.VMEM((tm + 8, dff), F32)],
        compiler_params=_cparams(("parallel", "arbitrary")),
    )(gate, gate, up, x1, cw, cb, wd)


def _ffn_sample_kernel(g_ref, s0_ref, s1_ref, u_ref, x1_ref, cw_ref, cb_ref, wd_ref, o_ref):
    o_ref[...] = _ffn_tail(s0_ref[...], s1_ref[...], g_ref[...], u_ref[...], x1_ref[...], cw_ref, cb_ref, wd_ref)


def _ffn_sample(gate, s0, s1, up, x1, cw, cb, wd):
    rows, dff = gate.shape
    dm = wd.shape[1]
    full = lambda shape: pl.BlockSpec(shape, lambda i: (0, 0))
    return pl.pallas_call(
        _ffn_sample_kernel, grid=(1,),
        in_specs=[full((rows, dff))] * 4 + [full((rows, dm)), full((CONV_W, dff)), full((1, dff)), full((dff, dm))],
        out_specs=full((rows, dm)),
        out_shape=jax.ShapeDtypeStruct((rows, dm), F32),
        compiler_params=_cparams(("arbitrary",)),
    )(gate, s0, s1, up, x1, cw, cb, wd)


def _rope_tables(pos, head_dim, rot_dim, theta):
    half = rot_dim // 2
    inv = jnp.exp(-math.log(theta) * jnp.arange(half, dtype=F32) / half)
    ang = pos[:, None] * inv[None, :]
    cos, sin = jnp.cos(ang), jnp.sin(ang)
    t = pos.shape[0]
    rest = head_dim - rot_dim
    c = jnp.concatenate([cos, cos, jnp.ones((t, rest), F32)], axis=1)
    s1 = jnp.concatenate([-sin, jnp.zeros((t, half + rest), F32)], axis=1)
    s2 = jnp.concatenate([jnp.zeros((t, half), F32), sin, jnp.zeros((t, rest), F32)], axis=1)
    rep = LANE // head_dim
    return [jnp.tile(a, (1, rep)) for a in (c, s1, s2)]


def _all_tables(pos):
    return (_rope_tables(pos, D_RET, D_RET, RET_THETA)
            + _rope_tables(pos, D_DSA, D_DSA // ROT_FRAC, ROPE_THETA)
            + _rope_tables(pos, D_DIFF, D_DIFF // ROT_FRAC, ROPE_THETA))


def _block_ones(width):
    i = jnp.arange(LANE)
    return (i[:, None] // width == i[None, :] // width).astype(BF16)


def _pad_cols(a, width):
    return jnp.pad(a, [(0, 0)] * (a.ndim - 1) + [(0, width - a.shape[-1])])


def _prep_w_in(w_in):
    parts, off = [], 0
    for size, width in zip(PROJ_SIZES, GROUP_WIDTHS):
        parts.append(_pad_cols(w_in[..., off:off + size], width))
        off += size
    return jnp.concatenate(parts, axis=-1).astype(BF16)


def _prep_w_out(w_out):
    a = w_out[:, :H_RET * D_RET]
    d = jnp.pad(w_out[:, H_RET * D_RET:H_RET * D_RET + MIXW], ((0, 0), (0, W3 - MIXW), (0, 0)))
    c = jnp.pad(w_out[:, H_RET * D_RET + MIXW:], ((0, 0), (0, W3 - MIXW), (0, 0)))
    return a.astype(BF16), d.astype(BF16), c.astype(BF16)


def _tile_gain(g):
    return jnp.tile(g, (1, LANE // g.shape[-1]))[:, None, :]


def kernel(x_prompt, x_sample, cache_dsa_k, cache_dsa_v, cache_dsa_kidx, cache_diff_k, cache_diff_v, state_ret, state_conv, page_table, meta, norm_mix_g, w_in, ret_gn_g, dsa_qn_g, dsa_kn_g, diff_qn_g, diff_kn_g, diff_lq1, diff_lk1, diff_lq2, diff_lk2, diff_subln_g, w_out, norm_ffn_g, ffn_w_gate, ffn_w_up, ffn_conv_w, ffn_conv_b, ffn_w_down):
    nb, seq, dm = x_prompt.shape
    ns, ts, _ = x_sample.shape
    assert ts == 1
    depth = w_in.shape[0]
    dff = ffn_w_gate.shape[-1]
    pool = cache_dsa_k.shape[1]
    npg = page_table.shape[1]
    past = npg * PAGE_SIZE
    tp = seq + N_META
    tpad = -(-tp // K_CHUNK) * K_CHUNK
    k_p = min(TOPK_MAX, tp // 4)
    k_s = min(TOPK_MAX, (past + ts) // 4)
    tm_p = ROW_TILE
    tm_s = ns
    assert ns % SAMPLE_TILE == 0 and ns % 8 == 0

    w_in_p = _prep_w_in(w_in)
    wo_a, wo_d, wo_c = _prep_w_out(w_out)
    wg16, wu16, wd16 = ffn_w_gate.astype(BF16), ffn_w_up.astype(BF16), ffn_w_down.astype(BF16)
    gmix = norm_mix_g[:, None, :]
    gffn = norm_ffn_g[:, None, :]
    qn2, kn2 = _tile_gain(dsa_qn_g), _tile_gain(dsa_kn_g)
    fqn4, fkn4 = _tile_gain(diff_qn_g), _tile_gain(diff_kn_g)
    sub2 = _tile_gain(diff_subln_g)
    gn_flat = ret_gn_g[:, None, :]
    gn_heads = ret_gn_g.reshape(depth, H_RET, D_RET)
    b64, b32 = _block_ones(64), _block_ones(32)
    tabs_p = _all_tables(jnp.arange(tpad, dtype=F32))
    tabs_s = _all_tables(jnp.full((ns,), float(past), F32))
    cb = ffn_conv_b[:, None, :]

    lam_all = (jnp.exp(jnp.sum(diff_lq1 * diff_lk1, axis=-1)) - jnp.exp(jnp.sum(diff_lq2 * diff_lk2, axis=-1)))

    lane_head = jnp.arange(W3) // V_DIFF
    rows16 = jnp.arange(16)
    pick_dsa = jnp.stack([(rows16[:, None] == lane_head[None, :]), jnp.zeros((16, W3), bool)]).astype(F32)
    pick_diff = jnp.stack([(rows16[:, None] == 2 * lane_head[None, :]),
                           (rows16[:, None] == 2 * lane_head[None, :] + 1)]).astype(F32)

    kidx_pages = cache_dsa_kidx.reshape(depth * pool, PAGE_SIZE, D_IDX)
    def mxu_pages(cache):
        flat = cache.reshape(depth * pool, PAGE_SIZE, MIXW).astype(BF16)
        return jnp.pad(flat, ((0, 0), (0, 0), (0, W3 - MIXW)))

    dsak_pages, dsav_pages = mxu_pages(cache_dsa_k), mxu_pages(cache_dsa_v)
    difk_pages, difv_pages = mxu_pages(cache_diff_k), mxu_pages(cache_diff_v)
    state4 = state_ret.reshape(depth * ns, H_RET, D_RET, D_RET)
    pt_flat = page_table.reshape(-1).astype(I32)

    xp = jnp.concatenate([jnp.broadcast_to(meta.astype(x_prompt.dtype), (nb, N_META, dm)), x_prompt], axis=1)
    xp = jnp.pad(xp, ((0, 0), (0, tpad - tp), (0, 0))).reshape(nb * tpad, dm)
    xs = x_sample.reshape(ns, dm)

    outs_p = [[] for _ in range(7)]
    outs_s = [[] for _ in range(7)]
    for l in range(depth):
        lam_init = 0.8 - 0.6 * math.exp(-0.3 * l)
        scal = jnp.stack([lam_all[l] + lam_init, jnp.asarray(1.0 - lam_init, F32)]).astype(F32)
        gains = (qn2[l], kn2[l], fqn4[l], fkn4[l])

        (ret4, bq16, bk16, bv16, iq16, ik16, iw, cq16, ck16, cv16, bk32, bv32, ik32, ck32, cv32) = _inproj(
            xp, gmix[l], w_in_p[l], tabs_p, gains, (b64, b32), tm_p)
        a_p, s_p = _ret_prompt(ret4, gn_flat[l], b64, nb, tpad, tp)
        d_p = _dsa_prompt(bq16, iq16, iw, ik16, bk16, bv16, nb, tpad, tp, k_p)
        c_p = _diff_prompt(scal, cq16, ck16, cv16, sub2[l], b64, nb, tpad, tp)
        x1, gate, up = _outproj(xp, a_p, d_p, c_p, wo_a[l], wo_d[l], wo_c[l], gffn[l], wg16[l], wu16[l], tm_p)
        xp = _ffn_prompt(gate, up, x1, ffn_conv_w[l], cb[l], wd16[l], nb, tpad, tm_p)

        def rows_p(a):
            return a.reshape(nb, tpad, -1)[:, :tp]

        outs_p[0].append(s_p)
        outs_p[1].append(gate.reshape(nb, tpad, dff)[:, tp - (CONV_W - 1):tp])
        outs_p[2].append(rows_p(bk32).reshape(nb, tp, H_DSA, D_DSA))
        outs_p[3].append(rows_p(bv32).reshape(nb, tp, H_DSA, D_DSA))
        outs_p[4].append(rows_p(ik32))
        outs_p[5].append(rows_p(ck32).reshape(nb, tp, H_DIFF, 2 * D_DIFF))
        outs_p[6].append(rows_p(cv32).reshape(nb, tp, H_DIFF, V_DIFF))

        (ret4, bq16, bk16, bv16, iq16, ik16, iw, cq16, ck16, cv16, bk32, bv32, ik32, ck32, cv32) = _inproj(
            xs, gmix[l], w_in_p[l], tabs_s, gains, (b64, b32), tm_s)
        nblk = ns // SAMPLE_TILE
        to_cols = lambda a: a.reshape(nblk, SAMPLE_TILE, W3).transpose(0, 2, 1)
        a3, s_s = _ret_sample(to_cols(ret4[:, O_RQ:O_RQ + W3]), to_cols(ret4[:, O_RK:O_RK + W3]),
                              ret4[:, O_RV:O_RV + W3].reshape(ns, H_RET, D_RET),
                              ret4[:, O_RG:O_RG + W3].reshape(ns, H_RET, D_RET), gn_heads[l], state4, l, ns)
        a_s = a3.reshape(ns, W3)
        iq3 = jnp.pad(iq16.reshape(ns, H_IDX, D_IDX), ((0, 0), (0, 16 - H_IDX), (0, 0)))
        iwb = jnp.pad(jnp.broadcast_to(iw[:, :H_IDX, None], (ns, H_IDX, LANE)), ((0, 0), (0, 16 - H_IDX), (0, 0)))
        sc_pages, sc_new = _idx_sample(pt_flat, iq3, iwb, kidx_pages, ik32[:, None, :], l, pool, ns, npg)
        n_keys = past + ts
        kc_s = K_CHUNK
        nkc_s = -(-n_keys // kc_s)
        scores = jnp.concatenate([sc_pages.reshape(ns, past), sc_new[:, 0, :1],
                                  jnp.full((ns, nkc_s * kc_s - n_keys), NEG, F32)], axis=1)
        bias = _select_sample(scores.reshape(ns, nkc_s, kc_s).transpose(1, 0, 2), k_s)
        bias = bias.transpose(1, 0, 2).reshape(ns, nkc_s * kc_s)
        grp = _page_group(npg)
        bias_pages = bias[:, :past].reshape(ns, npg // grp, grp * PAGE_SIZE)
        bias_new = jnp.broadcast_to(bias[:, past:past + 1, None], (ns, 1, LANE))
        q_dsa = jnp.where(pick_dsa[0][None] > 0, bq16[:, None, :], 0).astype(BF16)
        pad8 = lambda a: _pad_cols(a, W3)[:, None, :]
        d_s = _paged_attn(pt_flat, scal, q_dsa, dsak_pages, dsav_pages, bias_pages, pad8(bk32), pad8(bv32), bias_new,
                          pick_dsa, sub2[l], b64, l, pool, ns, npg, False).reshape(ns, W3)
        lane_sub = (jnp.arange(W3) % V_DIFF) // D_DIFF
        pick_q = (rows16[:, None] == (2 * lane_head + lane_sub)[None, :])
        q_dif = jnp.where(pick_q[None], cq16[:, None, :], 0).astype(BF16)
        zero_pages = jnp.zeros_like(bias_pages)
        zero_new = jnp.zeros((ns, 1, LANE), F32)
        c_s = _paged_attn(pt_flat, scal, q_dif, difk_pages, difv_pages, zero_pages, pad8(ck32), pad8(cv32), zero_new,
                          pick_diff, sub2[l], b64, l, pool, ns, npg, True).reshape(ns, W3)
        x1, gate, up = _outproj(xs, a_s, d_s, c_s, wo_a[l], wo_d[l], wo_c[l], gffn[l], wg16[l], wu16[l], tm_s)
        xs = _ffn_sample(gate, state_conv[l, :, 0], state_conv[l, :, 1], up, x1, ffn_conv_w[l], cb[l], wd16[l])

        outs_s[0].append(s_s)
        outs_s[1].append(jnp.stack([state_conv[l, :, 1], gate], axis=1))
        outs_s[2].append(bk32.reshape(ns, ts, H_DSA, D_DSA))
        outs_s[3].append(bv32.reshape(ns, ts, H_DSA, D_DSA))
        outs_s[4].append(ik32.reshape(ns, ts, D_IDX))
        outs_s[5].append(ck32.reshape(ns, ts, H_DIFF, 2 * D_DIFF))
        outs_s[6].append(cv32.reshape(ns, ts, H_DIFF, V_DIFF))

    ret_p, conv_p, dsak_p, dsav_p, kidx_p, diffk_p, diffv_p = (jnp.stack(a) for a in outs_p)
    ret_s, conv_s, dsak_s, dsav_s, kidx_s, diffk_s, diffv_s = (jnp.stack(a) for a in outs_s)
    y_prompt = xp.reshape(nb, tpad, dm)[:, N_META:tp]
    y_sample = xs.reshape(ns, ts, dm)
    return (y_prompt, y_sample, ret_p, ret_s, conv_p, conv_s, dsak_p, dsak_s, dsav_p, dsav_s,
            kidx_p, kidx_s, diffk_p, diffk_s, diffv_p, diffv_s)
```

```python
import functools
import math
import struct

import jax
import jax.numpy as jnp
from jax import lax
from jax.experimental import pallas as pl
from jax.experimental.pallas import tpu as pltpu

F32 = jnp.float32
BF16 = jnp.bfloat16
I32 = jnp.int32

N_META = 16
H_RET, D_RET = 6, 64
H_DSA, D_DSA = 5, 64
H_IDX, D_IDX = 8, 64
TOPK_MAX = 256
H_DIFF, D_DIFF, V_DIFF = 5, 32, 64
CONV_W = 3
ROPE_THETA = 500000.0
RET_THETA = 10000.0
ROT_FRAC = 4
PAGE_SIZE = 128
NEG = -1e30
EPS = 1e-6
LOG2E = math.log2(math.e)
PROJ_SIZES = ((H_RET * D_RET,) * 4 + (H_DSA * D_DSA,) * 3 + (H_IDX * D_IDX, D_IDX, H_IDX)
              + (2 * H_DIFF * D_DIFF, 2 * H_DIFF * D_DIFF, H_DIFF * V_DIFF))

LANE = 128
VMEM_LIMIT = 56 * 1024 * 1024

W3 = 3 * LANE
GROUP_WIDTHS = (W3, W3, W3, W3, W3, W3, W3, H_IDX * D_IDX, LANE, LANE, W3, W3, W3)
GROUP_OFFS = tuple(int(sum(GROUP_WIDTHS[:i])) for i in range(len(GROUP_WIDTHS)))
IN_PAD = int(sum(GROUP_WIDTHS))
(O_RQ, O_RK, O_RV, O_RG, O_BQ, O_BK, O_BV, O_IQ, O_IK, O_IW, O_CQ, O_CK, O_CV) = GROUP_OFFS
MIXW = H_DSA * D_DSA

ROW_TILE = 256
INPROJ_TILE = 512
Q_TILE = 128
K_CHUNK = 512
RET_CHUNK = 128
SAMPLE_TILE = 16


def _float_key(v):
    b = struct.unpack("<i", struct.pack("<f", v))[0]
    return b if b >= 0 else b ^ 0x7FFFFFFF


NEG_KEY = _float_key(NEG)
INT_MAX = 2 ** 31 - 1
INT_MIN = -2 ** 31


def _cparams(sem):
    return pltpu.CompilerParams(dimension_semantics=sem, vmem_limit_bytes=VMEM_LIMIT)


def _seq_spec(shape, index_map):
    return pl.BlockSpec(shape, index_map, pipeline_mode=pl.Buffered(1))


def _dot(a, b):
    return jnp.dot(a, b, preferred_element_type=F32)


def _dot_nt(a, b):
    return lax.dot_general(a, b, (((1,), (1,)), ((), ())), preferred_element_type=F32)


def _silu(x):
    return x * (1.0 / (1.0 + jnp.exp(-x)))


def _seg_rsqrt(x, seg_ref, width):
    x2 = x * x
    hi = x2.astype(BF16)
    lo = (x2 - hi.astype(F32)).astype(BF16)
    seg = seg_ref[...]
    ss = _dot(hi, seg) + _dot(lo, seg)
    return lax.rsqrt(ss * (1.0 / width) + EPS)


def _rope_chunk(x, c, s1, s2, half):
    return x * c + pltpu.roll(x, LANE - half, 1) * s1 + pltpu.roll(x, half, 1) * s2


def _lane_mask(shape, lo, hi):
    lane = lax.broadcasted_iota(I32, shape, len(shape) - 1)
    return (lane >= lo) & (lane < hi)


def _store_value_variants(ref, pair, v):
    low = _lane_mask(v.shape, 0, LANE // 2)
    ref[:, (2 * pair) * LANE:(2 * pair + 1) * LANE] = jnp.where(low, v, 1.0).astype(BF16)
    ref[:, (2 * pair + 1) * LANE:(2 * pair + 2) * LANE] = jnp.where(low, 1.0, v).astype(BF16)


def _value_cols(head):
    return slice(head * LANE, (head + 1) * LANE)


def _inproj_kernel(x_ref, g_ref, w_ref, rc_ref, rs1_ref, rs2_ref, dc_ref, ds1_ref, ds2_ref,
                   fc_ref, fs1_ref, fs2_ref, qn_ref, kn_ref, fqn_ref, fkn_ref, b64_ref, b32_ref,
                   ret4_ref, bq16_ref, bk16_ref, bv16_ref, iq16_ref, ik16_ref, iw_ref,
                   cq16_ref, ck16_ref, cv16_ref, bk32_ref, bv32_ref, ik32_ref, ck32_ref, cv32_ref):
    x = x_ref[...]
    h = x * lax.rsqrt(jnp.mean(x * x, axis=-1, keepdims=True) + EPS) * g_ref[...]
    hb = h.astype(BF16)

    def proj(off, j):
        return _dot(hb, w_ref[:, off + j * LANE: off + (j + 1) * LANE])

    ret_tab = (rc_ref[...], rs1_ref[...], rs2_ref[...], D_RET // 2)
    dsa_tab = (dc_ref[...], ds1_ref[...], ds2_ref[...], D_DSA // ROT_FRAC // 2)
    dif_tab = (fc_ref[...], fs1_ref[...], fs2_ref[...], D_DIFF // ROT_FRAC // 2)

    def rope(p, tab):
        return _rope_chunk(p, *tab)

    for j in range(3):
        sl = slice(j * LANE, (j + 1) * LANE)
        ret4_ref[:, O_RQ + j * LANE: O_RQ + (j + 1) * LANE] = rope(proj(O_RQ, j), ret_tab)
        ret4_ref[:, O_RK + j * LANE: O_RK + (j + 1) * LANE] = rope(proj(O_RK, j), ret_tab) * (D_RET ** -0.5)
        ret4_ref[:, O_RV + j * LANE: O_RV + (j + 1) * LANE] = proj(O_RV, j)
        ret4_ref[:, O_RG + j * LANE: O_RG + (j + 1) * LANE] = proj(O_RG, j)
        p = proj(O_BQ, j)
        bq = rope(p * _seg_rsqrt(p, b64_ref, D_DSA) * qn_ref[...], dsa_tab) * (D_DSA ** -0.5 * LOG2E)
        bq16_ref[:, sl] = bq.astype(BF16)
        p = proj(O_BK, j)
        bk = rope(p * _seg_rsqrt(p, b64_ref, D_DSA) * kn_ref[...], dsa_tab)
        bk16_ref[:, sl] = bk.astype(BF16)
        bv = proj(O_BV, j)
        _store_value_variants(bv16_ref, j, bv)
        p = proj(O_CQ, j)
        cq = rope(p * _seg_rsqrt(p, b32_ref, D_DIFF) * fqn_ref[...], dif_tab) * (D_DIFF ** -0.5 * LOG2E)
        cq16_ref[:, sl] = cq.astype(BF16)
        p = proj(O_CK, j)
        ck = rope(p * _seg_rsqrt(p, b32_ref, D_DIFF) * fkn_ref[...], dif_tab)
        ck16_ref[:, sl] = ck.astype(BF16)
        cv = proj(O_CV, j)
        _store_value_variants(cv16_ref, j, cv)
        w = LANE if j < 2 else MIXW - 2 * LANE
        bk32_ref[:, j * LANE: j * LANE + w] = bk[:, :w]
        bv32_ref[:, j * LANE: j * LANE + w] = bv[:, :w]
        ck32_ref[:, j * LANE: j * LANE + w] = ck[:, :w]
        cv32_ref[:, j * LANE: j * LANE + w] = cv[:, :w]
    for j in range(H_IDX * D_IDX // LANE):
        iq16_ref[:, j * LANE:(j + 1) * LANE] = rope(proj(O_IQ, j), dsa_tab).astype(BF16)
    ik = rope(proj(O_IK, 0), dsa_tab)
    ik32_ref[...] = ik[:, :D_IDX]
    ik16_ref[...] = jnp.where(_lane_mask(ik.shape, 0, D_IDX), ik, pltpu.roll(ik, D_IDX, 1)).astype(BF16)
    iw_ref[...] = proj(O_IW, 0) * ((H_IDX * D_IDX) ** -0.5)


def _inproj(x2d, gmix, w, tabs, gains, segs, tm):
    rows, dm = x2d.shape
    tab_rows = tabs[0].shape[0]
    nb_tab = tab_rows // tm
    grid = (rows // tm,)
    row = lambda i: (i, 0)
    const = lambda i: (0, 0)
    tab = lambda i: (i % nb_tab, 0)
    in_specs = ([pl.BlockSpec((tm, dm), row), pl.BlockSpec((1, dm), const), pl.BlockSpec((dm, IN_PAD), const)]
                + [pl.BlockSpec((tm, LANE), tab)] * 9
                + [pl.BlockSpec((1, LANE), const)] * 4
                + [pl.BlockSpec((LANE, LANE), const)] * 2)
    widths = [(4 * W3, F32), (W3, BF16), (W3, BF16), (2 * W3, BF16), (H_IDX * D_IDX, BF16), (LANE, BF16), (LANE, F32),
              (W3, BF16), (W3, BF16), (2 * W3, BF16), (MIXW, F32), (MIXW, F32), (D_IDX, F32), (MIXW, F32), (MIXW, F32)]
    out_shape = [jax.ShapeDtypeStruct((rows, wd), dt) for wd, dt in widths]
    out_specs = [pl.BlockSpec((tm, wd), row) for wd, _ in widths]
    return pl.pallas_call(
        _inproj_kernel, grid=grid, in_specs=in_specs, out_specs=out_specs, out_shape=out_shape,
        compiler_params=_cparams(("parallel",)),
    )(x2d, gmix, w, *tabs, *gains, *segs)


def _ret_lg(h):
    return math.log(1.0 - 2.0 ** (-5.0 - h))


def _gate(o, g, gn, b64_ref):
    return _silu(g) * (o * _seg_rsqrt(o, b64_ref, D_RET) * gn)


def _ret_prompt_kernel(ret4_ref, gn_ref, b64_ref, a_ref, sfin_ref, s_ref, *, tp):
    c = pl.program_id(1)
    ch = RET_CHUNK

    @pl.when(c == 0)
    def _():
        s_ref[...] = jnp.zeros_like(s_ref)

    valid = jnp.clip(tp - c * ch, 0, ch).astype(F32)
    ri = lax.broadcasted_iota(I32, (ch, 1), 0).astype(F32)
    rel = (lax.broadcasted_iota(I32, (ch, ch), 0) - lax.broadcasted_iota(I32, (ch, ch), 1)).astype(F32)
    lane = lax.broadcasted_iota(I32, (1, LANE), 1)
    rowi = lax.broadcasted_iota(I32, (LANE, 1), 0)
    for p in range(H_RET // 2):
        sl = slice(p * LANE, (p + 1) * LANE)
        lg_a, lg_b = _ret_lg(2 * p), _ret_lg(2 * p + 1)
        q = ret4_ref[:, O_RQ + p * LANE: O_RQ + (p + 1) * LANE]
        k = ret4_ref[:, O_RK + p * LANE: O_RK + (p + 1) * LANE]
        v = ret4_ref[:, O_RV + p * LANE: O_RV + (p + 1) * LANE]
        k16 = k.astype(BF16)
        v16 = v.astype(BF16)
        s_pair = s_ref[p]
        s16 = s_pair.astype(BF16)
        outs = []
        for half, lg in ((0, lg_a), (1, lg_b)):
            qm = jnp.where(_lane_mask(q.shape, half * D_RET, (half + 1) * D_RET), q, 0.0).astype(BF16)
            dmat = jnp.where(rel >= 0.0, jnp.exp(jnp.maximum(rel, 0.0) * lg), 0.0)
            inner = _dot_nt(qm, k16) * dmat
            outs.append(_dot(inner.astype(BF16), v16) + _dot(qm, s16) * jnp.exp((ri + 1.0) * lg))
        o_pair = jnp.where(lane < D_RET, outs[0], outs[1])
        lgv = jnp.where(lane < D_RET, lg_a, lg_b)
        kw = k * jnp.where(ri < valid, jnp.exp((valid - 1.0 - ri) * lgv), 0.0)
        lgc = jnp.where(rowi < D_RET, lg_a, lg_b)
        s_new = jnp.exp(valid * lgc) * s_pair + _dot(kw.T.astype(BF16), v16)
        s_ref[p] = s_new
        g = ret4_ref[:, O_RG + p * LANE: O_RG + (p + 1) * LANE]
        a_ref[:, sl] = _gate(o_pair, g, gn_ref[:, sl], b64_ref)

    @pl.when(c == pl.num_programs(1) - 1)
    def _():
        for h in range(H_RET):
            blk = s_ref[h // 2]
            o = (h % 2) * D_RET
            sfin_ref[0, h] = blk[o:o + D_RET, o:o + D_RET]


def _ret_prompt(ret4, gn, b64, n, tpad, tp):
    nc = tpad // RET_CHUNK
    return pl.pallas_call(
        functools.partial(_ret_prompt_kernel, tp=tp),
        grid=(n, nc),
        in_specs=[pl.BlockSpec((RET_CHUNK, 4 * W3), lambda b, c: (b * nc + c, 0)),
                  pl.BlockSpec((1, W3), lambda b, c: (0, 0)),
                  pl.BlockSpec((LANE, LANE), lambda b, c: (0, 0))],
        out_specs=[pl.BlockSpec((RET_CHUNK, W3), lambda b, c: (b * nc + c, 0)),
                   pl.BlockSpec((1, H_RET, D_RET, D_RET), lambda b, c: (b, 0, 0, 0))],
        out_shape=[jax.ShapeDtypeStruct((n * tpad, W3), F32),
                   jax.ShapeDtypeStruct((n, H_RET, D_RET, D_RET), F32)],
        scratch_shapes=[pltpu.VMEM((H_RET // 2, LANE, LANE), F32)],
        compiler_params=_cparams(("parallel", "arbitrary")),
    )(ret4, gn, b64)


def _ret_sample_kernel(qt_ref, kt_ref, v_ref, g_ref, gn_ref, st_ref, a_ref, snew_ref, o_ref):
    nb = v_ref.shape[0]
    for j in range(nb):
        for h in range(H_RET):
            gamma = math.exp(_ret_lg(h))
            qcol = qt_ref[0, h * D_RET:(h + 1) * D_RET, j:j + 1]
            kcol = kt_ref[0, h * D_RET:(h + 1) * D_RET, j:j + 1]
            vrow = v_ref[j, h:h + 1, :]
            s_new = gamma * st_ref[j, h] + kcol * vrow
            snew_ref[j, h] = s_new
            o_ref[j, h:h + 1, :] = jnp.sum(qcol * s_new, axis=0, keepdims=True)
    o = o_ref[...]
    r = lax.rsqrt(jnp.mean(o * o, axis=-1, keepdims=True) + EPS)
    a_ref[...] = _silu(g_ref[...]) * (o * r * gn_ref[...])


def _ret_sample(qt, kt, v3, g3, gn2, state4, layer, ns):
    nb = qt.shape[2]
    nblk = ns // nb
    blk3 = pl.BlockSpec((nb, H_RET, D_RET), lambda i: (i, 0, 0))
    tsp = pl.BlockSpec((1, H_RET * D_RET, nb), lambda i: (i, 0, 0))
    return pl.pallas_call(
        _ret_sample_kernel,
        grid=(nblk,),
        in_specs=[tsp, tsp, blk3, blk3,
                  pl.BlockSpec((H_RET, D_RET), lambda i: (0, 0)),
                  pl.BlockSpec((nb, H_RET, D_RET, D_RET), lambda i: (layer * nblk + i, 0, 0, 0))],
        out_specs=[blk3, pl.BlockSpec((nb, H_RET, D_RET, D_RET), lambda i: (i, 0, 0, 0))],
        out_shape=[jax.ShapeDtypeStruct((ns, H_RET, D_RET), F32),
                   jax.ShapeDtypeStruct((ns, H_RET, D_RET, D_RET), F32)],
        scratch_shapes=[pltpu.VMEM((nb, H_RET, D_RET), F32)],
        compiler_params=_cparams(("parallel",)),
    )(qt, kt, v3, g3, gn2, state4)


def _to_key(s):
    b = lax.bitcast_convert_type(s, I32)
    return jnp.where(b >= 0, b, b ^ jnp.int32(0x7FFFFFFF))


def _fold_lanes(x):
    acc = x[:, :LANE]
    for t in range(1, x.shape[1] // LANE):
        acc = acc + x[:, t * LANE:(t + 1) * LANE]
    return acc


def _select_bias(keys_ref, bias_ref, nkc, k, pos_bits):
    _, rows, kc = keys_ref.shape
    groups = kc // LANE
    lane_pos = lax.broadcasted_iota(I32, (rows, LANE), 1)
    wide = lambda col: jnp.broadcast_to(col, (rows, LANE))

    def count(ones):
        def body(c, acc):
            for g in range(groups):
                acc = acc + ones(keys_ref[c, :, g * LANE:(g + 1) * LANE], c * kc + g * LANE + lane_pos)
            return acc
        acc = lax.fori_loop(0, nkc, body, jnp.zeros((rows, LANE), F32))
        return jnp.sum(acc, axis=-1, keepdims=True)

    def thr_step(it, thr):
        cand = thr + lax.shift_left(jnp.int32(1), 31 - it)
        cand_b = wide(cand)
        tot = count(lambda kk, kpos: jnp.where(kk >= cand_b, 1.0, 0.0))
        return jnp.where(tot >= k, cand, thr)

    thr = lax.fori_loop(0, 32, thr_step, jnp.full((rows, 1), INT_MIN, I32))
    thr_b = wide(thr)
    c_gt = count(lambda kk, kpos: jnp.where(kk > thr_b, 1.0, 0.0))
    c_ge = count(lambda kk, kpos: jnp.where(kk >= thr_b, 1.0, 0.0))
    need = k - c_gt
    excess = jnp.where(thr > NEG_KEY, c_ge - k, 0.0)

    def tie_search():
        def step(it, lo):
            cand = lo + lax.shift_left(jnp.int32(1), pos_bits - 1 - it)
            cand_b = wide(cand)
            f = count(lambda kk, kpos: jnp.where(kk == thr_b, jnp.where(kpos < cand_b, 1.0, 0.0), 0.0))
            return jnp.where(f < need, cand, lo)
        return lax.fori_loop(0, pos_bits, step, jnp.zeros((rows, 1), I32))

    last = lax.cond(jnp.max(excess) > 0.0, tie_search, lambda: jnp.full((rows, 1), INT_MAX, I32))
    last_b = wide(last)

    def emit(c, carry):
        for g in range(groups):
            cols = slice(g * LANE, (g + 1) * LANE)
            kk = keys_ref[c, :, cols]
            tie = jnp.where(c * kc + g * LANE + lane_pos <= last_b, 0.0, NEG)
            sel = jnp.where(kk > thr_b, 0.0, jnp.where(kk == thr_b, tie, NEG))
            bias_ref[c, :, cols] = jnp.where(kk > NEG_KEY, sel, NEG)
        return carry

    lax.fori_loop(0, nkc, emit, 0)


def _fold_max(x):
    acc = x[:, :LANE]
    for t in range(1, x.shape[1] // LANE):
        acc = jnp.maximum(acc, x[:, t * LANE:(t + 1) * LANE])
    return acc


def _stacked_scores(q_stack, sizes, k16):
    s = _dot_nt(q_stack, k16)
    rows = s.shape[0] // sizes
    return [s[i * rows:(i + 1) * rows] for i in range(sizes)]


def _softmax_pv(s_ref, n_chunks, maccs, v_groups):
    nmap, _, rows, kc = s_ref.shape
    mbs = [jnp.broadcast_to(jnp.max(m, axis=-1, keepdims=True), (rows, LANE)) for m in maccs]

    def probs(i, c):
        ps = [jnp.exp2(s_ref[i, c, :, g * LANE:(g + 1) * LANE] - mbs[i]) for g in range(kc // LANE)]
        return jnp.concatenate(ps, axis=1).astype(BF16)

    def pass_pv(c, accs):
        new = list(accs)
        for maps, v_chunk in v_groups:
            pv = _dot(jnp.concatenate([probs(i, c) for i in maps], axis=0), v_chunk(c))
            for t, i in enumerate(maps):
                new[i] = accs[i] + pv[t * rows:(t + 1) * rows]
        return tuple(new)

    accs = lax.fori_loop(0, n_chunks, pass_pv, tuple(jnp.zeros((rows, LANE), F32) for _ in range(nmap)))
    return [acc * (1.0 / pltpu.roll(acc, LANE // 2, 1)) for acc in accs]


def _online_update(carry, s, v16):
    m, l, acc = carry
    m_new = jnp.maximum(m, jnp.max(s, axis=-1, keepdims=True))
    alpha = jnp.exp2(m - m_new)
    p = jnp.exp2(s - m_new)
    l = alpha * l + jnp.sum(p, axis=-1, keepdims=True)
    acc = alpha * acc + _dot(p.astype(BF16), v16)
    return m_new, l, acc


def _online_init(rows, width):
    return (jnp.full((rows, 1), -jnp.inf, F32), jnp.zeros((rows, 1), F32), jnp.zeros((rows, width), F32))


def _dsa_prompt_kernel(bq_ref, iq_ref, iw_ref, ik_ref, bk_ref, bv_ref, o_ref, keys_ref, bias_ref, s_ref,
                       *, top_k, pos_bits, tp):
    q0 = pl.program_id(1) * Q_TILE

    @pl.when(q0 < tp)
    def _():
        _dsa_prompt_tile(bq_ref, iq_ref, iw_ref, ik_ref, bk_ref, bv_ref, o_ref, keys_ref, bias_ref, s_ref,
                         q0, top_k, pos_bits)

    @pl.when(q0 >= tp)
    def _():
        o_ref[...] = jnp.zeros_like(o_ref)


def _dsa_prompt_tile(bq_ref, iq_ref, iw_ref, ik_ref, bk_ref, bv_ref, o_ref, keys_ref, bias_ref, s_ref,
                     q0, top_k, pos_bits):
    tq, kc = Q_TILE, K_CHUNK
    nkc = (q0 + tq + kc - 1) // kc
    qpos = q0 + lax.broadcasted_iota(I32, (tq, 1), 0)
    iw = iw_ref[...]

    iq_heads = []
    for h in range(H_IDX):
        ch = iq_ref[:, (h // 2) * LANE:(h // 2 + 1) * LANE]
        iq_heads.append(jnp.where(_lane_mask(ch.shape, (h % 2) * D_IDX, (h % 2 + 1) * D_IDX), ch, 0).astype(BF16))

    def score_body(c, carry):
        kch = ik_ref[pl.ds(pl.multiple_of(c * kc, kc), kc), :]
        acc = jnp.zeros((tq, kc), F32)
        for h in range(H_IDX):
            acc = acc + iw[:, h:h + 1] * jnp.maximum(_dot_nt(iq_heads[h], kch), 0.0)
        acc = jnp.where(acc == 0.0, 0.0, acc)
        kpos = c * kc + lax.broadcasted_iota(I32, (1, kc), 1)
        keys_ref[c] = _to_key(jnp.where(kpos <= qpos, acc, NEG))
        return carry

    lax.fori_loop(0, nkc, score_body, 0)
    _select_bias(keys_ref, bias_ref, nkc, float(top_k), pos_bits)

    lane = lax.broadcasted_iota(I32, (1, LANE), 1)
    qms = []
    for h in range(H_DSA):
        ch = bq_ref[:, (h // 2) * LANE:(h // 2 + 1) * LANE]
        qms.append(jnp.where(_lane_mask(ch.shape, (h % 2) * D_DSA, (h % 2 + 1) * D_DSA), ch, 0).astype(BF16))

    pairs = [list(range(2 * p, min(2 * p + 2, H_DSA))) for p in range((H_DSA + 1) // 2)]
    q_stacks = [jnp.concatenate([qms[h] for h in hs], axis=0) for hs in pairs]

    def pass_qk(c, maccs):
        rows = pl.ds(pl.multiple_of(c * kc, kc), kc)
        bias = bias_ref[c]
        new = list(maccs)
        for p, hs in enumerate(pairs):
            blocks = _stacked_scores(q_stacks[p], len(hs), bk_ref[rows, p * LANE:(p + 1) * LANE])
            for h, s in zip(hs, blocks):
                s = s + bias
                s_ref[h, c] = s
                new[h] = jnp.maximum(maccs[h], _fold_max(s))
        return tuple(new)

    maccs = lax.fori_loop(0, nkc, pass_qk, tuple(jnp.full((tq, LANE), -jnp.inf, F32) for _ in range(H_DSA)))
    v_groups = [([h], lambda c, h=h: bv_ref[pl.ds(pl.multiple_of(c * kc, kc), kc), _value_cols(h)])
                for h in range(H_DSA)]
    outs = _softmax_pv(s_ref, nkc, maccs, v_groups)
    outs.append(jnp.zeros_like(outs[0]))
    for pr in range(3):
        o_ref[:, pr * LANE:(pr + 1) * LANE] = jnp.where(lane < D_DSA, outs[2 * pr], outs[2 * pr + 1])


def _dsa_prompt(bq16, iq16, iw, ik16, bk16, bv16, n, tpad, tp, top_k):
    nq = tpad // Q_TILE
    nkc = tpad // K_CHUNK
    pos_bits = max(1, int(tpad).bit_length())
    qrow = lambda b, i: (b * nq + i, 0)
    seq = lambda b, i: (b, 0)
    return pl.pallas_call(
        functools.partial(_dsa_prompt_kernel, top_k=top_k, pos_bits=pos_bits, tp=tp),
        grid=(n, nq),
        in_specs=[pl.BlockSpec((Q_TILE, W3), qrow), pl.BlockSpec((Q_TILE, H_IDX * D_IDX), qrow),
                  pl.BlockSpec((Q_TILE, LANE), qrow),
                  _seq_spec((tpad, LANE), seq), _seq_spec((tpad, W3), seq), _seq_spec((tpad, 2 * W3), seq)],
        out_specs=pl.BlockSpec((Q_TILE, W3), qrow),
        out_shape=jax.ShapeDtypeStruct((n * tpad, W3), F32),
        scratch_shapes=[pltpu.VMEM((nkc, Q_TILE, K_CHUNK), I32), pltpu.VMEM((nkc, Q_TILE, K_CHUNK), F32),
                        pltpu.VMEM((H_DSA, nkc, Q_TILE, K_CHUNK), F32)],
        compiler_params=_cparams(("parallel", "arbitrary")),
    )(bq16, iq16, iw, ik16, bk16, bv16)


def _diff_finish(o, post, sub, b64_ref):
    return o * _seg_rsqrt(o, b64_ref, V_DIFF) * sub * post


def _diff_prompt_kernel(sc_ref, cq_ref, ck_ref, cv_ref, sub_ref, b64_ref, o_ref, s_ref, *, tp):
    q0 = pl.program_id(1) * Q_TILE

    @pl.when(q0 < tp)
    def _():
        _diff_prompt_tile(sc_ref, cq_ref, ck_ref, cv_ref, sub_ref, b64_ref, o_ref, s_ref, q0)

    @pl.when(q0 >= tp)
    def _():
        o_ref[...] = jnp.zeros_like(o_ref)


def _diff_prompt_tile(sc_ref, cq_ref, ck_ref, cv_ref, sub_ref, b64_ref, o_ref, s_ref, q0):
    tq, kc = Q_TILE, K_CHUNK
    nfull = q0 // kc
    qpos = q0 + lax.broadcasted_iota(I32, (tq, 1), 0)
    lam, post = sc_ref[0], sc_ref[1]
    lane = lax.broadcasted_iota(I32, (1, LANE), 1)
    qmaps = []
    for r in range(2 * H_DIFF):
        ch = cq_ref[:, (r // 4) * LANE:(r // 4 + 1) * LANE]
        base = (r % 4) * D_DIFF
        qmaps.append(jnp.where(_lane_mask(ch.shape, base, base + D_DIFF), ch, 0).astype(BF16))

    ok = (nfull * kc + lax.broadcasted_iota(I32, (1, kc), 1)) <= qpos
    fin = []
    for grp in range(2):
        maps = list(range(grp * H_DIFF, (grp + 1) * H_DIFF))
        by_chunk = [[i for i, r in enumerate(maps) if r // 4 == ch] for ch in range(3)]
        by_chunk = [(ch, idx) for ch, idx in enumerate(by_chunk) if idx]
        q_stacks = [jnp.concatenate([qmaps[maps[i]] for i in idx], axis=0) for _, idx in by_chunk]
        by_head = [[i for i, r in enumerate(maps) if r // 2 == h] for h in range(H_DIFF)]
        v_groups = [(idx, lambda c, h=h: cv_ref[pl.ds(pl.multiple_of(c * kc, kc), kc), _value_cols(h)])
                    for h, idx in enumerate(by_head) if idx]

        def pass_qk(c, maccs, by_chunk=by_chunk, q_stacks=q_stacks, masked=False):
            rows = pl.ds(pl.multiple_of(c * kc, kc), kc)
            new = list(maccs)
            for (ch, idx), q_stack in zip(by_chunk, q_stacks):
                blocks = _stacked_scores(q_stack, len(idx), ck_ref[rows, ch * LANE:(ch + 1) * LANE])
                for i, s in zip(idx, blocks):
                    if masked:
                        s = jnp.where(ok, s, NEG)
                    s_ref[i, c] = s
                    new[i] = jnp.maximum(maccs[i], _fold_max(s))
            return tuple(new)

        maccs = lax.fori_loop(0, nfull, pass_qk, tuple(jnp.full((tq, LANE), -jnp.inf, F32) for _ in maps))
        maccs = pass_qk(nfull, maccs, masked=True)
        fin.extend(_softmax_pv(s_ref, nfull + 1, maccs, v_groups))
    heads = [fin[2 * h] - lam * fin[2 * h + 1] for h in range(H_DIFF)]
    heads.append(jnp.zeros_like(heads[0]))
    for pr in range(3):
        o = jnp.where(lane < V_DIFF, heads[2 * pr], heads[2 * pr + 1])
        o_ref[:, pr * LANE:(pr + 1) * LANE] = _diff_finish(o, post, sub_ref[...], b64_ref)


def _diff_prompt(scal, cq16, ck16, cv16, sub2, b64, n, tpad, tp):
    nq = tpad // Q_TILE
    qrow = lambda b, i, sc: (b * nq + i, 0)
    seq = lambda b, i, sc: (b, 0)
    const = lambda b, i, sc: (0, 0)
    return pl.pallas_call(
        functools.partial(_diff_prompt_kernel, tp=tp),
        grid_spec=pltpu.PrefetchScalarGridSpec(
            num_scalar_prefetch=1, grid=(n, nq),
            in_specs=[pl.BlockSpec((Q_TILE, W3), qrow), _seq_spec((tpad, W3), seq), _seq_spec((tpad, 2 * W3), seq),
                      pl.BlockSpec((1, LANE), const), pl.BlockSpec((LANE, LANE), const)],
            out_specs=pl.BlockSpec((Q_TILE, W3), qrow),
            scratch_shapes=[pltpu.VMEM((H_DIFF, tpad // K_CHUNK, Q_TILE, K_CHUNK), F32)]),
        out_shape=jax.ShapeDtypeStruct((n * tpad, W3), F32),
        compiler_params=_cparams(("parallel", "arbitrary")),
    )(scal, cq16, ck16, cv16, sub2, b64)


def _page_group(npg):
    return next(g for g in (8, 4, 2, 1) if npg % g == 0)


def _idx_sample_kernel(pt_ref, iq_ref, iwb_ref, *refs, group):
    kidx_refs = refs[:group]
    iknew_ref, sc_ref, scnew_ref = refs[group:]
    j = pl.program_id(1)
    iq16 = iq_ref[0]
    iw_col = iwb_ref[0][:, 0:1]
    rows = []
    for g in range(group):
        s = _dot_nt(iq16, kidx_refs[g][0].astype(BF16))
        rows.append(jnp.sum(iw_col * jnp.maximum(s, 0.0), axis=0, keepdims=True))
    row = jnp.concatenate(rows, axis=1)
    sc_ref[0, pl.ds(j, 1), :] = jnp.where(row == 0.0, 0.0, row)

    @pl.when(j == pl.num_programs(1) - 1)
    def _():
        ikn = iknew_ref[0].astype(BF16).astype(F32)
        t = jnp.sum(iq16.astype(F32) * ikn, axis=-1, keepdims=True)
        sn = jnp.sum(iwb_ref[0] * jnp.maximum(t, 0.0), axis=0, keepdims=True)
        scnew_ref[0] = jnp.where(sn == 0.0, 0.0, sn)


def _page_specs(shape, group, layer, pool, npg):
    def make(g):
        def index(n, j, pt, *_):
            return (layer * pool + pt[n * npg + j * group + g], 0, 0)
        return pl.BlockSpec(shape, index)
    return [make(g) for g in range(group)]


def _idx_sample(pt_flat, iq3, iwb, kidx_pages, iknew, layer, pool, ns, npg):
    group = _page_group(npg)
    nstep = npg // group
    per = lambda n, j, pt: (n, 0, 0)
    return pl.pallas_call(
        functools.partial(_idx_sample_kernel, group=group),
        grid_spec=pltpu.PrefetchScalarGridSpec(
            num_scalar_prefetch=1, grid=(ns, nstep),
            in_specs=[pl.BlockSpec((1, 16, D_IDX), per), pl.BlockSpec((1, 16, LANE), per)]
                     + _page_specs((1, PAGE_SIZE, D_IDX), group, layer, pool, npg)
                     + [pl.BlockSpec((1, 1, D_IDX), per)],
            out_specs=[pl.BlockSpec((1, nstep, group * PAGE_SIZE), per), pl.BlockSpec((1, 1, LANE), per)]),
        out_shape=[jax.ShapeDtypeStruct((ns, nstep, group * PAGE_SIZE), F32),
                   jax.ShapeDtypeStruct((ns, 1, LANE), F32)],
        compiler_params=_cparams(("parallel", "arbitrary")),
    )(pt_flat, iq3, iwb, *([kidx_pages] * group), iknew)


def _select_sample_kernel(sc_ref, bias_ref, keys_ref, *, top_k, pos_bits):
    nkc = keys_ref.shape[0]
    for c in range(nkc):
        keys_ref[c] = _to_key(sc_ref[c])
    _select_bias(keys_ref, bias_ref, nkc, float(top_k), pos_bits)


def _select_sample(scores3, top_k):
    nkc, rows, kc = scores3.shape
    pos_bits = max(1, int(nkc * kc).bit_length())
    full = pl.BlockSpec((nkc, rows, kc), lambda i: (0, 0, 0))
    return pl.pallas_call(
        functools.partial(_select_sample_kernel, top_k=top_k, pos_bits=pos_bits),
        grid=(1,), in_specs=[full], out_specs=full,
        out_shape=jax.ShapeDtypeStruct((nkc, rows, kc), F32),
        scratch_shapes=[pltpu.VMEM((nkc, rows, kc), I32)],
        compiler_params=_cparams(("arbitrary",)),
    )(scores3)


def _paged_attn_kernel(pt_ref, sc_ref, q_ref, *refs, group, diff):
    k_refs, v_refs = refs[:group], refs[group:2 * group]
    (bias_ref, knew_ref, vnew_ref, biasnew_ref, pick_ref, sub_ref, b64_ref, o_ref,
     m_ref, l_ref, acc_ref) = refs[2 * group:]
    j = pl.program_id(1)

    @pl.when(j == 0)
    def _():
        m_ref[...] = jnp.full_like(m_ref, -jnp.inf)
        l_ref[...] = jnp.zeros_like(l_ref)
        acc_ref[...] = jnp.zeros_like(acc_ref)

    q16 = q_ref[0]
    qk = q16[:, :MIXW]
    s = jnp.concatenate([_dot_nt(qk, k_refs[g][0]) for g in range(group)], axis=1) + bias_ref[0, pl.ds(j, 1), :]
    m_old = m_ref[...]
    m = jnp.maximum(m_old, jnp.max(s, axis=-1, keepdims=True))
    alpha = jnp.exp2(m_old - m)
    p = jnp.exp2(s - m)
    l = alpha * l_ref[...] + jnp.sum(p, axis=-1, keepdims=True)
    p16 = p.astype(BF16)
    pv = _dot(p16[:, :PAGE_SIZE], v_refs[0][0])
    for g in range(1, group):
        pv = pv + _dot(p16[:, g * PAGE_SIZE:(g + 1) * PAGE_SIZE], v_refs[g][0])
    acc_ref[...] = alpha * acc_ref[...]
    acc_ref[:, :MIXW] += pv
    acc = acc_ref[...]
    m_ref[...], l_ref[...] = m, l

    @pl.when(j == pl.num_programs(1) - 1)
    def _():
        sn = jnp.sum(q16.astype(F32) * knew_ref[0], axis=-1, keepdims=True) + biasnew_ref[0][:, 0:1]
        mf = jnp.maximum(m, sn)
        af = jnp.exp2(m - mf)
        pn = jnp.exp2(sn - mf)
        lf = af * l + pn
        accf = af * acc + pn * vnew_ref[0]
        o = accf * (1.0 / lf)
        if diff:
            lam, post = sc_ref[0], sc_ref[1]
            o1 = jnp.sum(o * pick_ref[0], axis=0, keepdims=True)
            o2 = jnp.sum(o * pick_ref[1], axis=0, keepdims=True)
            od = o1 - lam * o2
            for pr in range(3):
                ch = od[:, pr * LANE:(pr + 1) * LANE]
                o_ref[0, :, pr * LANE:(pr + 1) * LANE] = (
                    ch * _seg_rsqrt(ch, b64_ref, V_DIFF) * sub_ref[...] * post)
        else:
            o_ref[0] = jnp.sum(o * pick_ref[0], axis=0, keepdims=True)


def _paged_attn(pt_flat, scal, qmaps, k_pages, v_pages, bias3, knew, vnew, biasnew, pick, sub2, b64,
                layer, pool, ns, npg, diff):
    nmap = qmaps.shape[1]
    group = _page_group(npg)
    nstep = npg // group
    per = lambda n, j, pt, sc: (n, 0, 0)
    c2 = lambda n, j, pt, sc: (0, 0)
    c3 = lambda n, j, pt, sc: (0, 0, 0)
    pages = _page_specs((1, PAGE_SIZE, MIXW), group, layer, pool, npg)
    return pl.pallas_call(
        functools.partial(_paged_attn_kernel, group=group, diff=diff),
        grid_spec=pltpu.PrefetchScalarGridSpec(
            num_scalar_prefetch=2, grid=(ns, nstep),
            in_specs=[pl.BlockSpec((1, nmap, W3), per)] + pages + pages
                     + [pl.BlockSpec((1, nstep, group * PAGE_SIZE), per),
                        pl.BlockSpec((1, 1, W3), per), pl.BlockSpec((1, 1, W3), per),
                        pl.BlockSpec((1, 1, LANE), per),
                        pl.BlockSpec((2, nmap, W3), c3), pl.BlockSpec((1, LANE), c2), pl.BlockSpec((LANE, LANE), c2)],
            out_specs=pl.BlockSpec((1, 1, W3), per),
            scratch_shapes=[pltpu.VMEM((nmap, 1), F32), pltpu.VMEM((nmap, 1), F32), pltpu.VMEM((nmap, W3), F32)]),
        out_shape=jax.ShapeDtypeStruct((ns, 1, W3), F32),
        compiler_params=_cparams(("parallel", "arbitrary")),
    )(pt_flat, scal, qmaps, *([k_pages] * group), *([v_pages] * group), bias3, knew, vnew, biasnew, pick, sub2, b64)


def _outproj_kernel(x_ref, a_ref, d_ref, c_ref, wa_ref, wd_ref, wc_ref, g_ref, wg_ref, wu_ref,
                    x1_ref, gate_ref, up_ref):
    x1 = (x_ref[...] + _dot(a_ref[...].astype(BF16), wa_ref[...]) + _dot(d_ref[...].astype(BF16), wd_ref[...])
          + _dot(c_ref[...].astype(BF16), wc_ref[...]))
    x1_ref[...] = x1
    h = (x1 * lax.rsqrt(jnp.mean(x1 * x1, axis=-1, keepdims=True) + EPS) * g_ref[...]).astype(BF16)
    gate_ref[...] = _dot(h, wg_ref[...])
    up_ref[...] = _dot(h, wu_ref[...])


def _outproj(x2d, a, d, c, wa, wd_, wc, gffn, wg, wu, tm):
    rows, dm = x2d.shape
    dff = wg.shape[1]
    row = lambda i: (i, 0)
    const = lambda i: (0, 0)
    return pl.pallas_call(
        _outproj_kernel, grid=(rows // tm,),
        in_specs=[pl.BlockSpec((tm, dm), row)] + [pl.BlockSpec((tm, W3), row)] * 3
                 + [pl.BlockSpec((W3, dm), const)] * 3
                 + [pl.BlockSpec((1, dm), const), pl.BlockSpec((dm, dff), const), pl.BlockSpec((dm, dff), const)],
        out_specs=[pl.BlockSpec((tm, dm), row), pl.BlockSpec((tm, dff), row), pl.BlockSpec((tm, dff), row)],
        out_shape=[jax.ShapeDtypeStruct((rows, dm), F32), jax.ShapeDtypeStruct((rows, dff), F32),
                   jax.ShapeDtypeStruct((rows, dff), F32)],
        compiler_params=_cparams(("parallel",)),
    )(x2d, a, d, c, wa, wd_, wc, gffn, wg, wu)


def _ffn_tail(gm2, gm1, g0, u, x1, cw_ref, cb_ref, wd_ref):
    gc = cw_ref[0:1, :] * gm2 + cw_ref[1:2, :] * gm1 + cw_ref[2:3, :] * g0 + cb_ref[...]
    return x1 + _dot((_silu(gc) * u).astype(BF16), wd_ref[...])


def _ffn_prompt_kernel(g_ref, halo_ref, u_ref, x1_ref, cw_ref, cb_ref, wd_ref, o_ref, gs_ref):
    i = pl.program_id(1)
    tm = g_ref.shape[0]
    halo = halo_ref[...]
    gs_ref[0:8, :] = jnp.where(i == 0, jnp.zeros_like(halo), halo)
    gs_ref[8:8 + tm, :] = g_ref[...]
    o_ref[...] = _ffn_tail(gs_ref[6:6 + tm, :], gs_ref[7:7 + tm, :], g_ref[...], u_ref[...], x1_ref[...],
                           cw_ref, cb_ref, wd_ref)


def _ffn_prompt(gate, up, x1, cw, cb, wd, n, tpad, tm):
    dff, dm = wd.shape
    nb = tpad // tm
    row = lambda b, i: (b * nb + i, 0)
    halo = lambda b, i: (jnp.maximum((b * nb + i) * (tm // 8) - 1, 0), 0)
    const = lambda b, i: (0, 0)
    return pl.pallas_call(
        _ffn_prompt_kernel, grid=(n, nb),
        in_specs=[pl.BlockSpec((tm, dff), row), pl.BlockSpec((8, dff), halo), pl.BlockSpec((tm, dff), row),
                  pl.BlockSpec((tm, dm), row), pl.BlockSpec((CONV_W, dff), const), pl.BlockSpec((1, dff), const),
                  pl.BlockSpec((dff, dm), const)],
        out_specs=pl.BlockSpec((tm, dm), row),
        out_shape=jax.ShapeDtypeStruct((n * tpad, dm), F32),
        scratch_shapes=[pltpu.VMEM((tm + 8, dff), F32)],
        compiler_params=_cparams(("parallel", "arbitrary")),
    )(gate, gate, up, x1, cw, cb, wd)


def _ffn_sample_kernel(g_ref, s0_ref, s1_ref, u_ref, x1_ref, cw_ref, cb_ref, wd_ref, o_ref):
    o_ref[...] = _ffn_tail(s0_ref[...], s1_ref[...], g_ref[...], u_ref[...], x1_ref[...], cw_ref, cb_ref, wd_ref)


def _ffn_sample(gate, s0, s1, up, x1, cw, cb, wd):
    rows, dff = gate.shape
    dm = wd.shape[1]
    full = lambda shape: pl.BlockSpec(shape, lambda i: (0, 0))
    return pl.pallas_call(
        _ffn_sample_kernel, grid=(1,),
        in_specs=[full((rows, dff))] * 4 + [full((rows, dm)), full((CONV_W, dff)), full((1, dff)), full((dff, dm))],
        out_specs=full((rows, dm)),
        out_shape=jax.ShapeDtypeStruct((rows, dm), F32),
        compiler_params=_cparams(("arbitrary",)),
    )(gate, s0, s1, up, x1, cw, cb, wd)


def _rope_tables(pos, head_dim, rot_dim, theta):
    half = rot_dim // 2
    inv = jnp.exp(-math.log(theta) * jnp.arange(half, dtype=F32) / half)
    ang = pos[:, None] * inv[None, :]
    cos, sin = jnp.cos(ang), jnp.sin(ang)
    t = pos.shape[0]
    rest = head_dim - rot_dim
    c = jnp.concatenate([cos, cos, jnp.ones((t, rest), F32)], axis=1)
    s1 = jnp.concatenate([-sin, jnp.zeros((t, half + rest), F32)], axis=1)
    s2 = jnp.concatenate([jnp.zeros((t, half), F32), sin, jnp.zeros((t, rest), F32)], axis=1)
    rep = LANE // head_dim
    return [jnp.tile(a, (1, rep)) for a in (c, s1, s2)]


def _all_tables(pos):
    return (_rope_tables(pos, D_RET, D_RET, RET_THETA)
            + _rope_tables(pos, D_DSA, D_DSA // ROT_FRAC, ROPE_THETA)
            + _rope_tables(pos, D_DIFF, D_DIFF // ROT_FRAC, ROPE_THETA))


def _block_ones(width):
    i = jnp.arange(LANE)
    return (i[:, None] // width == i[None, :] // width).astype(BF16)


def _pad_cols(a, width):
    return jnp.pad(a, [(0, 0)] * (a.ndim - 1) + [(0, width - a.shape[-1])])


def _prep_w_in(w_in):
    parts, off = [], 0
    for size, width in zip(PROJ_SIZES, GROUP_WIDTHS):
        parts.append(_pad_cols(w_in[..., off:off + size], width))
        off += size
    return jnp.concatenate(parts, axis=-1).astype(BF16)


def _prep_w_out(w_out):
    a = w_out[:, :H_RET * D_RET]
    d = jnp.pad(w_out[:, H_RET * D_RET:H_RET * D_RET + MIXW], ((0, 0), (0, W3 - MIXW), (0, 0)))
    c = jnp.pad(w_out[:, H_RET * D_RET + MIXW:], ((0, 0), (0, W3 - MIXW), (0, 0)))
    return a.astype(BF16), d.astype(BF16), c.astype(BF16)


def _tile_gain(g):
    return jnp.tile(g, (1, LANE // g.shape[-1]))[:, None, :]


def kernel(x_prompt, x_sample, cache_dsa_k, cache_dsa_v, cache_dsa_kidx, cache_diff_k, cache_diff_v, state_ret, state_conv, page_table, meta, norm_mix_g, w_in, ret_gn_g, dsa_qn_g, dsa_kn_g, diff_qn_g, diff_kn_g, diff_lq1, diff_lk1, diff_lq2, diff_lk2, diff_subln_g, w_out, norm_ffn_g, ffn_w_gate, ffn_w_up, ffn_conv_w, ffn_conv_b, ffn_w_down):
    nb, seq, dm = x_prompt.shape
    ns, ts, _ = x_sample.shape
    assert ts == 1
    depth = w_in.shape[0]
    dff = ffn_w_gate.shape[-1]
    pool = cache_dsa_k.shape[1]
    npg = page_table.shape[1]
    past = npg * PAGE_SIZE
    tp = seq + N_META
    tpad = -(-tp // K_CHUNK) * K_CHUNK
    k_p = min(TOPK_MAX, tp // 4)
    k_s = min(TOPK_MAX, (past + ts) // 4)
    tm_p = ROW_TILE
    tm_s = ns
    assert ns % SAMPLE_TILE == 0 and ns % 8 == 0

    w_in_p = _prep_w_in(w_in)
    wo_a, wo_d, wo_c = _prep_w_out(w_out)
    wg16, wu16, wd16 = ffn_w_gate.astype(BF16), ffn_w_up.astype(BF16), ffn_w_down.astype(BF16)
    gmix = norm_mix_g[:, None, :]
    gffn = norm_ffn_g[:, None, :]
    qn2, kn2 = _tile_gain(dsa_qn_g), _tile_gain(dsa_kn_g)
    fqn4, fkn4 = _tile_gain(diff_qn_g), _tile_gain(diff_kn_g)
    sub2 = _tile_gain(diff_subln_g)
    gn_flat = ret_gn_g[:, None, :]
    gn_heads = ret_gn_g.reshape(depth, H_RET, D_RET)
    b64, b32 = _block_ones(64), _block_ones(32)
    tabs_p = _all_tables(jnp.arange(tpad, dtype=F32))
    tabs_s = _all_tables(jnp.full((ns,), float(past), F32))
    cb = ffn_conv_b[:, None, :]

    lam_all = (jnp.exp(jnp.sum(diff_lq1 * diff_lk1, axis=-1)) - jnp.exp(jnp.sum(diff_lq2 * diff_lk2, axis=-1)))

    lane_head = jnp.arange(W3) // V_DIFF
    rows16 = jnp.arange(16)
    pick_dsa = jnp.stack([(rows16[:, None] == lane_head[None, :]), jnp.zeros((16, W3), bool)]).astype(F32)
    pick_diff = jnp.stack([(rows16[:, None] == 2 * lane_head[None, :]),
                           (rows16[:, None] == 2 * lane_head[None, :] + 1)]).astype(F32)

    kidx_pages = cache_dsa_kidx.reshape(depth * pool, PAGE_SIZE, D_IDX)
    def mxu_pages(cache):
        return cache.reshape(depth * pool, PAGE_SIZE, MIXW).astype(BF16)

    dsak_pages, dsav_pages = mxu_pages(cache_dsa_k), mxu_pages(cache_dsa_v)
    difk_pages, difv_pages = mxu_pages(cache_diff_k), mxu_pages(cache_diff_v)
    state4 = state_ret.reshape(depth * ns, H_RET, D_RET, D_RET)
    pt_flat = page_table.reshape(-1).astype(I32)

    xp = jnp.concatenate([jnp.broadcast_to(meta.astype(x_prompt.dtype), (nb, N_META, dm)), x_prompt], axis=1)
    xp = jnp.pad(xp, ((0, 0), (0, tpad - tp), (0, 0))).reshape(nb * tpad, dm)
    xs = x_sample.reshape(ns, dm)

    outs_p = [[] for _ in range(7)]
    outs_s = [[] for _ in range(7)]
    for l in range(depth):
        lam_init = 0.8 - 0.6 * math.exp(-0.3 * l)
        scal = jnp.stack([lam_all[l] + lam_init, jnp.asarray(1.0 - lam_init, F32)]).astype(F32)
        gains = (qn2[l], kn2[l], fqn4[l], fkn4[l])

        (ret4, bq16, bk16, bv16, iq16, ik16, iw, cq16, ck16, cv16, bk32, bv32, ik32, ck32, cv32) = _inproj(
            xp, gmix[l], w_in_p[l], tabs_p, gains, (b64, b32), INPROJ_TILE)
        a_p, s_p = _ret_prompt(ret4, gn_flat[l], b64, nb, tpad, tp)
        d_p = _dsa_prompt(bq16, iq16, iw, ik16, bk16, bv16, nb, tpad, tp, k_p)
        c_p = _diff_prompt(scal, cq16, ck16, cv16, sub2[l], b64, nb, tpad, tp)
        x1, gate, up = _outproj(xp, a_p, d_p, c_p, wo_a[l], wo_d[l], wo_c[l], gffn[l], wg16[l], wu16[l], tm_p)
        xp = _ffn_prompt(gate, up, x1, ffn_conv_w[l], cb[l], wd16[l], nb, tpad, tm_p)

        def rows_p(a):
            return a.reshape(nb, tpad, -1)[:, :tp]

        outs_p[0].append(s_p)
        outs_p[1].append(gate.reshape(nb, tpad, dff)[:, tp - (CONV_W - 1):tp])
        outs_p[2].append(rows_p(bk32).reshape(nb, tp, H_DSA, D_DSA))
        outs_p[3].append(rows_p(bv32).reshape(nb, tp, H_DSA, D_DSA))
        outs_p[4].append(rows_p(ik32))
        outs_p[5].append(rows_p(ck32).reshape(nb, tp, H_DIFF, 2 * D_DIFF))
        outs_p[6].append(rows_p(cv32).reshape(nb, tp, H_DIFF, V_DIFF))

        (ret4, bq16, bk16, bv16, iq16, ik16, iw, cq16, ck16, cv16, bk32, bv32, ik32, ck32, cv32) = _inproj(
            xs, gmix[l], w_in_p[l], tabs_s, gains, (b64, b32), tm_s)
        nblk = ns // SAMPLE_TILE
        to_cols = lambda a: a.reshape(nblk, SAMPLE_TILE, W3).transpose(0, 2, 1)
        a3, s_s = _ret_sample(to_cols(ret4[:, O_RQ:O_RQ + W3]), to_cols(ret4[:, O_RK:O_RK + W3]),
                              ret4[:, O_RV:O_RV + W3].reshape(ns, H_RET, D_RET),
                              ret4[:, O_RG:O_RG + W3].reshape(ns, H_RET, D_RET), gn_heads[l], state4, l, ns)
        a_s = a3.reshape(ns, W3)
        iq3 = jnp.pad(iq16.reshape(ns, H_IDX, D_IDX), ((0, 0), (0, 16 - H_IDX), (0, 0)))
        iwb = jnp.pad(jnp.broadcast_to(iw[:, :H_IDX, None], (ns, H_IDX, LANE)), ((0, 0), (0, 16 - H_IDX), (0, 0)))
        sc_pages, sc_new = _idx_sample(pt_flat, iq3, iwb, kidx_pages, ik32[:, None, :], l, pool, ns, npg)
        n_keys = past + ts
        kc_s = K_CHUNK
        nkc_s = -(-n_keys // kc_s)
        scores = jnp.concatenate([sc_pages.reshape(ns, past), sc_new[:, 0, :1],
                                  jnp.full((ns, nkc_s * kc_s - n_keys), NEG, F32)], axis=1)
        bias = _select_sample(scores.reshape(ns, nkc_s, kc_s).transpose(1, 0, 2), k_s)
        bias = bias.transpose(1, 0, 2).reshape(ns, nkc_s * kc_s)
        grp = _page_group(npg)
        bias_pages = bias[:, :past].reshape(ns, npg // grp, grp * PAGE_SIZE)
        bias_new = jnp.broadcast_to(bias[:, past:past + 1, None], (ns, 1, LANE))
        q_dsa = jnp.where(pick_dsa[0][None] > 0, bq16[:, None, :], 0).astype(BF16)
        pad8 = lambda a: _pad_cols(a, W3)[:, None, :]
        d_s = _paged_attn(pt_flat, scal, q_dsa, dsak_pages, dsav_pages, bias_pages, pad8(bk32), pad8(bv32), bias_new,
                          pick_dsa, sub2[l], b64, l, pool, ns, npg, False).reshape(ns, W3)
        lane_sub = (jnp.arange(W3) % V_DIFF) // D_DIFF
        pick_q = (rows16[:, None] == (2 * lane_head + lane_sub)[None, :])
        q_dif = jnp.where(pick_q[None], cq16[:, None, :], 0).astype(BF16)
        zero_pages = jnp.zeros_like(bias_pages)
        zero_new = jnp.zeros((ns, 1, LANE), F32)
        c_s = _paged_attn(pt_flat, scal, q_dif, difk_pages, difv_pages, zero_pages, pad8(ck32), pad8(cv32), zero_new,
                          pick_diff, sub2[l], b64, l, pool, ns, npg, True).reshape(ns, W3)
        x1, gate, up = _outproj(xs, a_s, d_s, c_s, wo_a[l], wo_d[l], wo_c[l], gffn[l], wg16[l], wu16[l], tm_s)
        xs = _ffn_sample(gate, state_conv[l, :, 0], state_conv[l, :, 1], up, x1, ffn_conv_w[l], cb[l], wd16[l])

        outs_s[0].append(s_s)
        outs_s[1].append(jnp.stack([state_conv[l, :, 1], gate], axis=1))
        outs_s[2].append(bk32.reshape(ns, ts, H_DSA, D_DSA))
        outs_s[3].append(bv32.reshape(ns, ts, H_DSA, D_DSA))
        outs_s[4].append(ik32.reshape(ns, ts, D_IDX))
        outs_s[5].append(ck32.reshape(ns, ts, H_DIFF, 2 * D_DIFF))
        outs_s[6].append(cv32.reshape(ns, ts, H_DIFF, V_DIFF))

    ret_p, conv_p, dsak_p, dsav_p, kidx_p, diffk_p, diffv_p = (jnp.stack(a) for a in outs_p)
    ret_s, conv_s, dsak_s, dsav_s, kidx_s, diffk_s, diffv_s = (jnp.stack(a) for a in outs_s)
    y_prompt = xp.reshape(nb, tpad, dm)[:, N_META:tp]
    y_sample = xs.reshape(ns, ts, dm)
    return (y_prompt, y_sample, ret_p, ret_s, conv_p, conv_s, dsak_p, dsak_s, dsav_p, dsav_s,
            kidx_p, kidx_s, diffk_p, diffk_s, diffv_p, diffv_s)
```

```python
import functools
import math
import struct

import jax
import jax.numpy as jnp
from jax import lax
from jax.experimental import pallas as pl
from jax.experimental.pallas import tpu as pltpu

F32 = jnp.float32
BF16 = jnp.bfloat16
I32 = jnp.int32

N_META = 16
H_RET, D_RET = 6, 64
H_DSA, D_DSA = 5, 64
H_IDX, D_IDX = 8, 64
TOPK_MAX = 256
H_DIFF, D_DIFF, V_DIFF = 5, 32, 64
CONV_W = 3
ROPE_THETA = 500000.0
RET_THETA = 10000.0
ROT_FRAC = 4
PAGE_SIZE = 128
NEG = -1e30
EPS = 1e-6
LOG2E = math.log2(math.e)
PROJ_SIZES = ((H_RET * D_RET,) * 4 + (H_DSA * D_DSA,) * 3 + (H_IDX * D_IDX, D_IDX, H_IDX)
              + (2 * H_DIFF * D_DIFF, 2 * H_DIFF * D_DIFF, H_DIFF * V_DIFF))

LANE = 128
VMEM_LIMIT = 56 * 1024 * 1024

W3 = 3 * LANE
GROUP_WIDTHS = (W3, W3, W3, W3, W3, W3, W3, H_IDX * D_IDX, LANE, LANE, W3, W3, W3)
GROUP_OFFS = tuple(int(sum(GROUP_WIDTHS[:i])) for i in range(len(GROUP_WIDTHS)))
IN_PAD = int(sum(GROUP_WIDTHS))
(O_RQ, O_RK, O_RV, O_RG, O_BQ, O_BK, O_BV, O_IQ, O_IK, O_IW, O_CQ, O_CK, O_CV) = GROUP_OFFS
MIXW = H_DSA * D_DSA

ROW_TILE = 256
Q_TILE = 128
K_CHUNK = 512
RET_CHUNK = 128
SAMPLE_TILE = 16


def _float_key(v):
    b = struct.unpack("<i", struct.pack("<f", v))[0]
    return b if b >= 0 else b ^ 0x7FFFFFFF


NEG_KEY = _float_key(NEG)
INT_MAX = 2 ** 31 - 1
INT_MIN = -2 ** 31


def _cparams(sem):
    return pltpu.CompilerParams(dimension_semantics=sem, vmem_limit_bytes=VMEM_LIMIT)


def _seq_spec(shape, index_map):
    return pl.BlockSpec(shape, index_map, pipeline_mode=pl.Buffered(1))


def _dot(a, b):
    return jnp.dot(a, b, preferred_element_type=F32)


def _dot_nt(a, b):
    return lax.dot_general(a, b, (((1,), (1,)), ((), ())), preferred_element_type=F32)


def _silu(x):
    return x * (1.0 / (1.0 + jnp.exp(-x)))


def _seg_rsqrt(x, seg_ref, width):
    x2 = x * x
    hi = x2.astype(BF16)
    lo = (x2 - hi.astype(F32)).astype(BF16)
    seg = seg_ref[...]
    ss = _dot(hi, seg) + _dot(lo, seg)
    return lax.rsqrt(ss * (1.0 / width) + EPS)


def _rope_chunk(x, c, s1, s2, half):
    return x * c + pltpu.roll(x, LANE - half, 1) * s1 + pltpu.roll(x, half, 1) * s2


def _lane_mask(shape, lo, hi):
    lane = lax.broadcasted_iota(I32, shape, len(shape) - 1)
    return (lane >= lo) & (lane < hi)


def _store_value_variants(ref, pair, v):
    low = _lane_mask(v.shape, 0, LANE // 2)
    ref[:, (2 * pair) * LANE:(2 * pair + 1) * LANE] = jnp.where(low, v, 1.0).astype(BF16)
    ref[:, (2 * pair + 1) * LANE:(2 * pair + 2) * LANE] = jnp.where(low, 1.0, v).astype(BF16)


def _value_cols(head):
    return slice(head * LANE, (head + 1) * LANE)


def _inproj_kernel(x_ref, g_ref, w_ref, rc_ref, rs1_ref, rs2_ref, dc_ref, ds1_ref, ds2_ref,
                   fc_ref, fs1_ref, fs2_ref, qn_ref, kn_ref, fqn_ref, fkn_ref, b64_ref, b32_ref,
                   ret4_ref, bq16_ref, bk16_ref, bv16_ref, iq16_ref, ik16_ref, iw_ref,
                   cq16_ref, ck16_ref, cv16_ref, bk32_ref, bv32_ref, ik32_ref, ck32_ref, cv32_ref):
    x = x_ref[...]
    h = x * lax.rsqrt(jnp.mean(x * x, axis=-1, keepdims=True) + EPS) * g_ref[...]
    hb = h.astype(BF16)

    def proj(off, j):
        return _dot(hb, w_ref[:, off + j * LANE: off + (j + 1) * LANE])

    ret_tab = (rc_ref[...], rs1_ref[...], rs2_ref[...], D_RET // 2)
    dsa_tab = (dc_ref[...], ds1_ref[...], ds2_ref[...], D_DSA // ROT_FRAC // 2)
    dif_tab = (fc_ref[...], fs1_ref[...], fs2_ref[...], D_DIFF // ROT_FRAC // 2)

    def rope(p, tab):
        return _rope_chunk(p, *tab)

    for j in range(3):
        sl = slice(j * LANE, (j + 1) * LANE)
        ret4_ref[:, O_RQ + j * LANE: O_RQ + (j + 1) * LANE] = rope(proj(O_RQ, j), ret_tab)
        ret4_ref[:, O_RK + j * LANE: O_RK + (j + 1) * LANE] = rope(proj(O_RK, j), ret_tab) * (D_RET ** -0.5)
        ret4_ref[:, O_RV + j * LANE: O_RV + (j + 1) * LANE] = proj(O_RV, j)
        ret4_ref[:, O_RG + j * LANE: O_RG + (j + 1) * LANE] = proj(O_RG, j)
        p = proj(O_BQ, j)
        bq = rope(p * _seg_rsqrt(p, b64_ref, D_DSA) * qn_ref[...], dsa_tab) * (D_DSA ** -0.5 * LOG2E)
        bq16_ref[:, sl] = bq.astype(BF16)
        p = proj(O_BK, j)
        bk = rope(p * _seg_rsqrt(p, b64_ref, D_DSA) * kn_ref[...], dsa_tab)
        bk16_ref[:, sl] = bk.astype(BF16)
        bv = proj(O_BV, j)
        _store_value_variants(bv16_ref, j, bv)
        p = proj(O_CQ, j)
        cq = rope(p * _seg_rsqrt(p, b32_ref, D_DIFF) * fqn_ref[...], dif_tab) * (D_DIFF ** -0.5 * LOG2E)
        cq16_ref[:, sl] = cq.astype(BF16)
        p = proj(O_CK, j)
        ck = rope(p * _seg_rsqrt(p, b32_ref, D_DIFF) * fkn_ref[...], dif_tab)
        ck16_ref[:, sl] = ck.astype(BF16)
        cv = proj(O_CV, j)
        _store_value_variants(cv16_ref, j, cv)
        w = LANE if j < 2 else MIXW - 2 * LANE
        bk32_ref[:, j * LANE: j * LANE + w] = bk[:, :w]
        bv32_ref[:, j * LANE: j * LANE + w] = bv[:, :w]
        ck32_ref[:, j * LANE: j * LANE + w] = ck[:, :w]
        cv32_ref[:, j * LANE: j * LANE + w] = cv[:, :w]
    for j in range(H_IDX * D_IDX // LANE):
        iq16_ref[:, j * LANE:(j + 1) * LANE] = rope(proj(O_IQ, j), dsa_tab).astype(BF16)
    ik = rope(proj(O_IK, 0), dsa_tab)
    ik32_ref[...] = ik[:, :D_IDX]
    ik16_ref[...] = jnp.where(_lane_mask(ik.shape, 0, D_IDX), ik, pltpu.roll(ik, D_IDX, 1)).astype(BF16)
    iw_ref[...] = proj(O_IW, 0) * ((H_IDX * D_IDX) ** -0.5)


def _inproj(x2d, gmix, w, tabs, gains, segs, tm):
    rows, dm = x2d.shape
    tab_rows = tabs[0].shape[0]
    nb_tab = tab_rows // tm
    grid = (rows // tm,)
    row = lambda i: (i, 0)
    const = lambda i: (0, 0)
    tab = lambda i: (i % nb_tab, 0)
    in_specs = ([pl.BlockSpec((tm, dm), row), pl.BlockSpec((1, dm), const), pl.BlockSpec((dm, IN_PAD), const)]
                + [pl.BlockSpec((tm, LANE), tab)] * 9
                + [pl.BlockSpec((1, LANE), const)] * 4
                + [pl.BlockSpec((LANE, LANE), const)] * 2)
    widths = [(4 * W3, F32), (W3, BF16), (W3, BF16), (2 * W3, BF16), (H_IDX * D_IDX, BF16), (LANE, BF16), (LANE, F32),
              (W3, BF16), (W3, BF16), (2 * W3, BF16), (MIXW, F32), (MIXW, F32), (D_IDX, F32), (MIXW, F32), (MIXW, F32)]
    out_shape = [jax.ShapeDtypeStruct((rows, wd), dt) for wd, dt in widths]
    out_specs = [pl.BlockSpec((tm, wd), row) for wd, _ in widths]
    return pl.pallas_call(
        _inproj_kernel, grid=grid, in_specs=in_specs, out_specs=out_specs, out_shape=out_shape,
        compiler_params=_cparams(("parallel",)),
    )(x2d, gmix, w, *tabs, *gains, *segs)


def _ret_lg(h):
    return math.log(1.0 - 2.0 ** (-5.0 - h))


def _gate(o, g, gn, b64_ref):
    return _silu(g) * (o * _seg_rsqrt(o, b64_ref, D_RET) * gn)


def _ret_prompt_kernel(ret4_ref, gn_ref, b64_ref, a_ref, sfin_ref, s_ref, *, tp):
    c = pl.program_id(1)
    ch = RET_CHUNK

    @pl.when(c == 0)
    def _():
        s_ref[...] = jnp.zeros_like(s_ref)

    valid = jnp.clip(tp - c * ch, 0, ch).astype(F32)
    ri = lax.broadcasted_iota(I32, (ch, 1), 0).astype(F32)
    rel = (lax.broadcasted_iota(I32, (ch, ch), 0) - lax.broadcasted_iota(I32, (ch, ch), 1)).astype(F32)
    lane = lax.broadcasted_iota(I32, (1, LANE), 1)
    rowi = lax.broadcasted_iota(I32, (LANE, 1), 0)
    for p in range(H_RET // 2):
        sl = slice(p * LANE, (p + 1) * LANE)
        lg_a, lg_b = _ret_lg(2 * p), _ret_lg(2 * p + 1)
        q = ret4_ref[:, O_RQ + p * LANE: O_RQ + (p + 1) * LANE]
        k = ret4_ref[:, O_RK + p * LANE: O_RK + (p + 1) * LANE]
        v = ret4_ref[:, O_RV + p * LANE: O_RV + (p + 1) * LANE]
        k16 = k.astype(BF16)
        v16 = v.astype(BF16)
        s_pair = s_ref[p]
        s16 = s_pair.astype(BF16)
        outs = []
        for half, lg in ((0, lg_a), (1, lg_b)):
            qm = jnp.where(_lane_mask(q.shape, half * D_RET, (half + 1) * D_RET), q, 0.0).astype(BF16)
            dmat = jnp.where(rel >= 0.0, jnp.exp(jnp.maximum(rel, 0.0) * lg), 0.0)
            inner = _dot_nt(qm, k16) * dmat
            outs.append(_dot(inner.astype(BF16), v16) + _dot(qm, s16) * jnp.exp((ri + 1.0) * lg))
        o_pair = jnp.where(lane < D_RET, outs[0], outs[1])
        lgv = jnp.where(lane < D_RET, lg_a, lg_b)
        kw = k * jnp.where(ri < valid, jnp.exp((valid - 1.0 - ri) * lgv), 0.0)
        lgc = jnp.where(rowi < D_RET, lg_a, lg_b)
        s_new = jnp.exp(valid * lgc) * s_pair + _dot(kw.T.astype(BF16), v16)
        s_ref[p] = s_new
        g = ret4_ref[:, O_RG + p * LANE: O_RG + (p + 1) * LANE]
        a_ref[:, sl] = _gate(o_pair, g, gn_ref[:, sl], b64_ref)

    @pl.when(c == pl.num_programs(1) - 1)
    def _():
        for h in range(H_RET):
            blk = s_ref[h // 2]
            o = (h % 2) * D_RET
            sfin_ref[0, h] = blk[o:o + D_RET, o:o + D_RET]


def _ret_prompt(ret4, gn, b64, n, tpad, tp):
    nc = tpad // RET_CHUNK
    return pl.pallas_call(
        functools.partial(_ret_prompt_kernel, tp=tp),
        grid=(n, nc),
        in_specs=[pl.BlockSpec((RET_CHUNK, 4 * W3), lambda b, c: (b * nc + c, 0)),
                  pl.BlockSpec((1, W3), lambda b, c: (0, 0)),
                  pl.BlockSpec((LANE, LANE), lambda b, c: (0, 0))],
        out_specs=[pl.BlockSpec((RET_CHUNK, W3), lambda b, c: (b * nc + c, 0)),
                   pl.BlockSpec((1, H_RET, D_RET, D_RET), lambda b, c: (b, 0, 0, 0))],
        out_shape=[jax.ShapeDtypeStruct((n * tpad, W3), F32),
                   jax.ShapeDtypeStruct((n, H_RET, D_RET, D_RET), F32)],
        scratch_shapes=[pltpu.VMEM((H_RET // 2, LANE, LANE), F32)],
        compiler_params=_cparams(("parallel", "arbitrary")),
    )(ret4, gn, b64)


def _ret_sample_kernel(qt_ref, kt_ref, v_ref, g_ref, gn_ref, st_ref, a_ref, snew_ref, o_ref):
    nb = v_ref.shape[0]
    for j in range(nb):
        for h in range(H_RET):
            gamma = math.exp(_ret_lg(h))
            qcol = qt_ref[0, h * D_RET:(h + 1) * D_RET, j:j + 1]
            kcol = kt_ref[0, h * D_RET:(h + 1) * D_RET, j:j + 1]
            vrow = v_ref[j, h:h + 1, :]
            s_new = gamma * st_ref[j, h] + kcol * vrow
            snew_ref[j, h] = s_new
            o_ref[j, h:h + 1, :] = jnp.sum(qcol * s_new, axis=0, keepdims=True)
    o = o_ref[...]
    r = lax.rsqrt(jnp.mean(o * o, axis=-1, keepdims=True) + EPS)
    a_ref[...] = _silu(g_ref[...]) * (o * r * gn_ref[...])


def _ret_sample(qt, kt, v3, g3, gn2, state4, layer, ns):
    nb = qt.shape[2]
    nblk = ns // nb
    blk3 = pl.BlockSpec((nb, H_RET, D_RET), lambda i: (i, 0, 0))
    tsp = pl.BlockSpec((1, H_RET * D_RET, nb), lambda i: (i, 0, 0))
    return pl.pallas_call(
        _ret_sample_kernel,
        grid=(nblk,),
        in_specs=[tsp, tsp, blk3, blk3,
                  pl.BlockSpec((H_RET, D_RET), lambda i: (0, 0)),
                  pl.BlockSpec((nb, H_RET, D_RET, D_RET), lambda i: (layer * nblk + i, 0, 0, 0))],
        out_specs=[blk3, pl.BlockSpec((nb, H_RET, D_RET, D_RET), lambda i: (i, 0, 0, 0))],
        out_shape=[jax.ShapeDtypeStruct((ns, H_RET, D_RET), F32),
                   jax.ShapeDtypeStruct((ns, H_RET, D_RET, D_RET), F32)],
        scratch_shapes=[pltpu.VMEM((nb, H_RET, D_RET), F32)],
        compiler_params=_cparams(("parallel",)),
    )(qt, kt, v3, g3, gn2, state4)


def _to_key(s):
    b = lax.bitcast_convert_type(s, I32)
    return jnp.where(b >= 0, b, b ^ jnp.int32(0x7FFFFFFF))


def _fold_lanes(x):
    acc = x[:, :LANE]
    for t in range(1, x.shape[1] // LANE):
        acc = acc + x[:, t * LANE:(t + 1) * LANE]
    return acc


def _select_bias(keys_ref, bias_ref, nkc, k, pos_bits):
    _, rows, kc = keys_ref.shape
    groups = kc // LANE
    lane_pos = lax.broadcasted_iota(I32, (rows, LANE), 1)
    wide = lambda col: jnp.broadcast_to(col, (rows, LANE))

    def count(ones):
        def body(c, acc):
            for g in range(groups):
                acc = acc + ones(keys_ref[c, :, g * LANE:(g + 1) * LANE], c * kc + g * LANE + lane_pos)
            return acc
        acc = lax.fori_loop(0, nkc, body, jnp.zeros((rows, LANE), F32))
        return jnp.sum(acc, axis=-1, keepdims=True)

    def thr_step(it, thr):
        cand = thr + lax.shift_left(jnp.int32(1), 31 - it)
        cand_b = wide(cand)
        tot = count(lambda kk, kpos: jnp.where(kk >= cand_b, 1.0, 0.0))
        return jnp.where(tot >= k, cand, thr)

    thr = lax.fori_loop(0, 32, thr_step, jnp.full((rows, 1), INT_MIN, I32))
    thr_b = wide(thr)
    c_gt = count(lambda kk, kpos: jnp.where(kk > thr_b, 1.0, 0.0))
    c_ge = count(lambda kk, kpos: jnp.where(kk >= thr_b, 1.0, 0.0))
    need = k - c_gt
    excess = jnp.where(thr > NEG_KEY, c_ge - k, 0.0)

    def tie_search():
        def step(it, lo):
            cand = lo + lax.shift_left(jnp.int32(1), pos_bits - 1 - it)
            cand_b = wide(cand)
            f = count(lambda kk, kpos: jnp.where(kk == thr_b, jnp.where(kpos < cand_b, 1.0, 0.0), 0.0))
            return jnp.where(f < need, cand, lo)
        return lax.fori_loop(0, pos_bits, step, jnp.zeros((rows, 1), I32))

    last = lax.cond(jnp.max(excess) > 0.0, tie_search, lambda: jnp.full((rows, 1), INT_MAX, I32))
    last_b = wide(last)

    def emit(c, carry):
        for g in range(groups):
            cols = slice(g * LANE, (g + 1) * LANE)
            kk = keys_ref[c, :, cols]
            tie = jnp.where(c * kc + g * LANE + lane_pos <= last_b, 0.0, NEG)
            sel = jnp.where(kk > thr_b, 0.0, jnp.where(kk == thr_b, tie, NEG))
            bias_ref[c, :, cols] = jnp.where(kk > NEG_KEY, sel, NEG)
        return carry

    lax.fori_loop(0, nkc, emit, 0)


def _fold_max(x):
    acc = x[:, :LANE]
    for t in range(1, x.shape[1] // LANE):
        acc = jnp.maximum(acc, x[:, t * LANE:(t + 1) * LANE])
    return acc


def _stacked_scores(q_stack, sizes, k16):
    s = _dot_nt(q_stack, k16)
    rows = s.shape[0] // sizes
    return [s[i * rows:(i + 1) * rows] for i in range(sizes)]


def _softmax_pv(s_ref, n_chunks, maccs, v_groups):
    nmap, _, rows, kc = s_ref.shape
    mbs = [jnp.broadcast_to(jnp.max(m, axis=-1, keepdims=True), (rows, LANE)) for m in maccs]

    def probs(i, c):
        ps = [jnp.exp2(s_ref[i, c, :, g * LANE:(g + 1) * LANE] - mbs[i]) for g in range(kc // LANE)]
        return jnp.concatenate(ps, axis=1).astype(BF16)

    def pass_pv(c, accs):
        new = list(accs)
        for maps, v_chunk in v_groups:
            pv = _dot(jnp.concatenate([probs(i, c) for i in maps], axis=0), v_chunk(c))
            for t, i in enumerate(maps):
                new[i] = accs[i] + pv[t * rows:(t + 1) * rows]
        return tuple(new)

    accs = lax.fori_loop(0, n_chunks, pass_pv, tuple(jnp.zeros((rows, LANE), F32) for _ in range(nmap)))
    return [acc * (1.0 / pltpu.roll(acc, LANE // 2, 1)) for acc in accs]


def _online_update(carry, s, v16):
    m, l, acc = carry
    m_new = jnp.maximum(m, jnp.max(s, axis=-1, keepdims=True))
    alpha = jnp.exp2(m - m_new)
    p = jnp.exp2(s - m_new)
    l = alpha * l + jnp.sum(p, axis=-1, keepdims=True)
    acc = alpha * acc + _dot(p.astype(BF16), v16)
    return m_new, l, acc


def _online_init(rows, width):
    return (jnp.full((rows, 1), -jnp.inf, F32), jnp.zeros((rows, 1), F32), jnp.zeros((rows, width), F32))


def _dsa_prompt_kernel(bq_ref, iq_ref, iw_ref, ik_ref, bk_ref, bv_ref, o_ref, keys_ref, bias_ref, s_ref,
                       *, top_k, pos_bits, tp):
    q0 = pl.program_id(1) * Q_TILE

    @pl.when(q0 < tp)
    def _():
        _dsa_prompt_tile(bq_ref, iq_ref, iw_ref, ik_ref, bk_ref, bv_ref, o_ref, keys_ref, bias_ref, s_ref,
                         q0, top_k, pos_bits)

    @pl.when(q0 >= tp)
    def _():
        o_ref[...] = jnp.zeros_like(o_ref)


def _dsa_prompt_tile(bq_ref, iq_ref, iw_ref, ik_ref, bk_ref, bv_ref, o_ref, keys_ref, bias_ref, s_ref,
                     q0, top_k, pos_bits):
    tq, kc = Q_TILE, K_CHUNK
    nkc = (q0 + tq + kc - 1) // kc
    qpos = q0 + lax.broadcasted_iota(I32, (tq, 1), 0)
    iw = iw_ref[...]

    iq_heads = []
    for h in range(H_IDX):
        ch = iq_ref[:, (h // 2) * LANE:(h // 2 + 1) * LANE]
        iq_heads.append(jnp.where(_lane_mask(ch.shape, (h % 2) * D_IDX, (h % 2 + 1) * D_IDX), ch, 0).astype(BF16))

    def score_body(c, carry):
        kch = ik_ref[pl.ds(pl.multiple_of(c * kc, kc), kc), :]
        acc = jnp.zeros((tq, kc), F32)
        for h in range(H_IDX):
            acc = acc + iw[:, h:h + 1] * jnp.maximum(_dot_nt(iq_heads[h], kch), 0.0)
        acc = jnp.where(acc == 0.0, 0.0, acc)
        kpos = c * kc + lax.broadcasted_iota(I32, (1, kc), 1)
        keys_ref[c] = _to_key(jnp.where(kpos <= qpos, acc, NEG))
        return carry

    lax.fori_loop(0, nkc, score_body, 0)
    _select_bias(keys_ref, bias_ref, nkc, float(top_k), pos_bits)

    lane = lax.broadcasted_iota(I32, (1, LANE), 1)
    qms = []
    for h in range(H_DSA):
        ch = bq_ref[:, (h // 2) * LANE:(h // 2 + 1) * LANE]
        qms.append(jnp.where(_lane_mask(ch.shape, (h % 2) * D_DSA, (h % 2 + 1) * D_DSA), ch, 0).astype(BF16))

    pairs = [list(range(2 * p, min(2 * p + 2, H_DSA))) for p in range((H_DSA + 1) // 2)]
    q_stacks = [jnp.concatenate([qms[h] for h in hs], axis=0) for hs in pairs]

    def pass_qk(c, maccs):
        rows = pl.ds(pl.multiple_of(c * kc, kc), kc)
        bias = bias_ref[c]
        new = list(maccs)
        for p, hs in enumerate(pairs):
            blocks = _stacked_scores(q_stacks[p], len(hs), bk_ref[rows, p * LANE:(p + 1) * LANE])
            for h, s in zip(hs, blocks):
                s = s + bias
                s_ref[h, c] = s
                new[h] = jnp.maximum(maccs[h], _fold_max(s))
        return tuple(new)

    maccs = lax.fori_loop(0, nkc, pass_qk, tuple(jnp.full((tq, LANE), -jnp.inf, F32) for _ in range(H_DSA)))
    v_groups = [([h], lambda c, h=h: bv_ref[pl.ds(pl.multiple_of(c * kc, kc), kc), _value_cols(h)])
                for h in range(H_DSA)]
    outs = _softmax_pv(s_ref, nkc, maccs, v_groups)
    outs.append(jnp.zeros_like(outs[0]))
    for pr in range(3):
        o_ref[:, pr * LANE:(pr + 1) * LANE] = jnp.where(lane < D_DSA, outs[2 * pr], outs[2 * pr + 1])


def _dsa_prompt(bq16, iq16, iw, ik16, bk16, bv16, n, tpad, tp, top_k):
    nq = tpad // Q_TILE
    nkc = tpad // K_CHUNK
    pos_bits = max(1, int(tpad).bit_length())
    qrow = lambda b, i: (b * nq + i, 0)
    seq = lambda b, i: (b, 0)
    return pl.pallas_call(
        functools.partial(_dsa_prompt_kernel, top_k=top_k, pos_bits=pos_bits, tp=tp),
        grid=(n, nq),
        in_specs=[pl.BlockSpec((Q_TILE, W3), qrow), pl.BlockSpec((Q_TILE, H_IDX * D_IDX), qrow),
                  pl.BlockSpec((Q_TILE, LANE), qrow),
                  _seq_spec((tpad, LANE), seq), _seq_spec((tpad, W3), seq), _seq_spec((tpad, 2 * W3), seq)],
        out_specs=pl.BlockSpec((Q_TILE, W3), qrow),
        out_shape=jax.ShapeDtypeStruct((n * tpad, W3), F32),
        scratch_shapes=[pltpu.VMEM((nkc, Q_TILE, K_CHUNK), I32), pltpu.VMEM((nkc, Q_TILE, K_CHUNK), F32),
                        pltpu.VMEM((H_DSA, nkc, Q_TILE, K_CHUNK), F32)],
        compiler_params=_cparams(("parallel", "arbitrary")),
    )(bq16, iq16, iw, ik16, bk16, bv16)


def _diff_finish(o, post, sub, b64_ref):
    return o * _seg_rsqrt(o, b64_ref, V_DIFF) * sub * post


def _diff_prompt_kernel(sc_ref, cq_ref, ck_ref, cv_ref, sub_ref, b64_ref, o_ref, s_ref, *, tp):
    q0 = pl.program_id(1) * Q_TILE

    @pl.when(q0 < tp)
    def _():
        _diff_prompt_tile(sc_ref, cq_ref, ck_ref, cv_ref, sub_ref, b64_ref, o_ref, s_ref, q0)

    @pl.when(q0 >= tp)
    def _():
        o_ref[...] = jnp.zeros_like(o_ref)


def _diff_prompt_tile(sc_ref, cq_ref, ck_ref, cv_ref, sub_ref, b64_ref, o_ref, s_ref, q0):
    tq, kc = Q_TILE, K_CHUNK
    nfull = q0 // kc
    qpos = q0 + lax.broadcasted_iota(I32, (tq, 1), 0)
    lam, post = sc_ref[0], sc_ref[1]
    lane = lax.broadcasted_iota(I32, (1, LANE), 1)
    qmaps = []
    for r in range(2 * H_DIFF):
        ch = cq_ref[:, (r // 4) * LANE:(r // 4 + 1) * LANE]
        base = (r % 4) * D_DIFF
        qmaps.append(jnp.where(_lane_mask(ch.shape, base, base + D_DIFF), ch, 0).astype(BF16))

    ok = (nfull * kc + lax.broadcasted_iota(I32, (1, kc), 1)) <= qpos
    fin = []
    for grp in range(2):
        maps = list(range(grp * H_DIFF, (grp + 1) * H_DIFF))
        by_chunk = [[i for i, r in enumerate(maps) if r // 4 == ch] for ch in range(3)]
        by_chunk = [(ch, idx) for ch, idx in enumerate(by_chunk) if idx]
        q_stacks = [jnp.concatenate([qmaps[maps[i]] for i in idx], axis=0) for _, idx in by_chunk]
        by_head = [[i for i, r in enumerate(maps) if r // 2 == h] for h in range(H_DIFF)]
        v_groups = [(idx, lambda c, h=h: cv_ref[pl.ds(pl.multiple_of(c * kc, kc), kc), _value_cols(h)])
                    for h, idx in enumerate(by_head) if idx]

        def pass_qk(c, maccs, by_chunk=by_chunk, q_stacks=q_stacks, masked=False):
            rows = pl.ds(pl.multiple_of(c * kc, kc), kc)
            new = list(maccs)
            for (ch, idx), q_stack in zip(by_chunk, q_stacks):
                blocks = _stacked_scores(q_stack, len(idx), ck_ref[rows, ch * LANE:(ch + 1) * LANE])
                for i, s in zip(idx, blocks):
                    if masked:
                        s = jnp.where(ok, s, NEG)
                    s_ref[i, c] = s
                    new[i] = jnp.maximum(maccs[i], _fold_max(s))
            return tuple(new)

        maccs = lax.fori_loop(0, nfull, pass_qk, tuple(jnp.full((tq, LANE), -jnp.inf, F32) for _ in maps))
        maccs = pass_qk(nfull, maccs, masked=True)
        fin.extend(_softmax_pv(s_ref, nfull + 1, maccs, v_groups))
    heads = [fin[2 * h] - lam * fin[2 * h + 1] for h in range(H_DIFF)]
    heads.append(jnp.zeros_like(heads[0]))
    for pr in range(3):
        o = jnp.where(lane < V_DIFF, heads[2 * pr], heads[2 * pr + 1])
        o_ref[:, pr * LANE:(pr + 1) * LANE] = _diff_finish(o, post, sub_ref[...], b64_ref)


def _diff_prompt(scal, cq16, ck16, cv16, sub2, b64, n, tpad, tp):
    nq = tpad // Q_TILE
    qrow = lambda b, i, sc: (b * nq + i, 0)
    seq = lambda b, i, sc: (b, 0)
    const = lambda b, i, sc: (0, 0)
    return pl.pallas_call(
        functools.partial(_diff_prompt_kernel, tp=tp),
        grid_spec=pltpu.PrefetchScalarGridSpec(
            num_scalar_prefetch=1, grid=(n, nq),
            in_specs=[pl.BlockSpec((Q_TILE, W3), qrow), _seq_spec((tpad, W3), seq), _seq_spec((tpad, 2 * W3), seq),
                      pl.BlockSpec((1, LANE), const), pl.BlockSpec((LANE, LANE), const)],
            out_specs=pl.BlockSpec((Q_TILE, W3), qrow),
            scratch_shapes=[pltpu.VMEM((H_DIFF, tpad // K_CHUNK, Q_TILE, K_CHUNK), F32)]),
        out_shape=jax.ShapeDtypeStruct((n * tpad, W3), F32),
        compiler_params=_cparams(("parallel", "arbitrary")),
    )(scal, cq16, ck16, cv16, sub2, b64)


def _page_group(npg):
    return next(g for g in (8, 4, 2, 1) if npg % g == 0)


def _idx_sample_kernel(pt_ref, iq_ref, iwb_ref, *refs, group):
    kidx_refs = refs[:group]
    iknew_ref, sc_ref, scnew_ref = refs[group:]
    j = pl.program_id(1)
    iq16 = iq_ref[0]
    iw_col = iwb_ref[0][:, 0:1]
    rows = []
    for g in range(group):
        s = _dot_nt(iq16, kidx_refs[g][0].astype(BF16))
        rows.append(jnp.sum(iw_col * jnp.maximum(s, 0.0), axis=0, keepdims=True))
    row = jnp.concatenate(rows, axis=1)
    sc_ref[0, pl.ds(j, 1), :] = jnp.where(row == 0.0, 0.0, row)

    @pl.when(j == pl.num_programs(1) - 1)
    def _():
        ikn = iknew_ref[0].astype(BF16).astype(F32)
        t = jnp.sum(iq16.astype(F32) * ikn, axis=-1, keepdims=True)
        sn = jnp.sum(iwb_ref[0] * jnp.maximum(t, 0.0), axis=0, keepdims=True)
        scnew_ref[0] = jnp.where(sn == 0.0, 0.0, sn)


def _page_specs(shape, group, layer, pool, npg):
    def make(g):
        def index(n, j, pt, *_):
            return (layer * pool + pt[n * npg + j * group + g], 0, 0)
        return pl.BlockSpec(shape, index)
    return [make(g) for g in range(group)]


def _idx_sample(pt_flat, iq3, iwb, kidx_pages, iknew, layer, pool, ns, npg):
    group = _page_group(npg)
    nstep = npg // group
    per = lambda n, j, pt: (n, 0, 0)
    return pl.pallas_call(
        functools.partial(_idx_sample_kernel, group=group),
        grid_spec=pltpu.PrefetchScalarGridSpec(
            num_scalar_prefetch=1, grid=(ns, nstep),
            in_specs=[pl.BlockSpec((1, 16, D_IDX), per), pl.BlockSpec((1, 16, LANE), per)]
                     + _page_specs((1, PAGE_SIZE, D_IDX), group, layer, pool, npg)
                     + [pl.BlockSpec((1, 1, D_IDX), per)],
            out_specs=[pl.BlockSpec((1, nstep, group * PAGE_SIZE), per), pl.BlockSpec((1, 1, LANE), per)]),
        out_shape=[jax.ShapeDtypeStruct((ns, nstep, group * PAGE_SIZE), F32),
                   jax.ShapeDtypeStruct((ns, 1, LANE), F32)],
        compiler_params=_cparams(("parallel", "arbitrary")),
    )(pt_flat, iq3, iwb, *([kidx_pages] * group), iknew)


def _select_sample_kernel(sc_ref, bias_ref, keys_ref, *, top_k, pos_bits):
    nkc = keys_ref.shape[0]
    for c in range(nkc):
        keys_ref[c] = _to_key(sc_ref[c])
    _select_bias(keys_ref, bias_ref, nkc, float(top_k), pos_bits)


def _select_sample(scores3, top_k):
    nkc, rows, kc = scores3.shape
    pos_bits = max(1, int(nkc * kc).bit_length())
    full = pl.BlockSpec((nkc, rows, kc), lambda i: (0, 0, 0))
    return pl.pallas_call(
        functools.partial(_select_sample_kernel, top_k=top_k, pos_bits=pos_bits),
        grid=(1,), in_specs=[full], out_specs=full,
        out_shape=jax.ShapeDtypeStruct((nkc, rows, kc), F32),
        scratch_shapes=[pltpu.VMEM((nkc, rows, kc), I32)],
        compiler_params=_cparams(("arbitrary",)),
    )(scores3)


def _paged_attn_kernel(pt_ref, sc_ref, q_ref, *refs, group, diff):
    k_refs, v_refs = refs[:group], refs[group:2 * group]
    (bias_ref, knew_ref, vnew_ref, biasnew_ref, pick_ref, sub_ref, b64_ref, o_ref,
     m_ref, l_ref, acc_ref) = refs[2 * group:]
    j = pl.program_id(1)

    @pl.when(j == 0)
    def _():
        m_ref[...] = jnp.full_like(m_ref, -jnp.inf)
        l_ref[...] = jnp.zeros_like(l_ref)
        acc_ref[...] = jnp.zeros_like(acc_ref)

    q16 = q_ref[0]
    s = jnp.concatenate([_dot_nt(q16, k_refs[g][0]) for g in range(group)], axis=1) + bias_ref[0, pl.ds(j, 1), :]
    m_old = m_ref[...]
    m = jnp.maximum(m_old, jnp.max(s, axis=-1, keepdims=True))
    alpha = jnp.exp2(m_old - m)
    p = jnp.exp2(s - m)
    l = alpha * l_ref[...] + jnp.sum(p, axis=-1, keepdims=True)
    p16 = p.astype(BF16)
    acc = alpha * acc_ref[...]
    for g in range(group):
        acc = acc + _dot(p16[:, g * PAGE_SIZE:(g + 1) * PAGE_SIZE], v_refs[g][0])
    m_ref[...], l_ref[...], acc_ref[...] = m, l, acc

    @pl.when(j == pl.num_programs(1) - 1)
    def _():
        sn = jnp.sum(q16.astype(F32) * knew_ref[0], axis=-1, keepdims=True) + biasnew_ref[0][:, 0:1]
        mf = jnp.maximum(m, sn)
        af = jnp.exp2(m - mf)
        pn = jnp.exp2(sn - mf)
        lf = af * l + pn
        accf = af * acc + pn * vnew_ref[0]
        o = accf * (1.0 / lf)
        if diff:
            lam, post = sc_ref[0], sc_ref[1]
            o1 = jnp.sum(o * pick_ref[0], axis=0, keepdims=True)
            o2 = jnp.sum(o * pick_ref[1], axis=0, keepdims=True)
            od = o1 - lam * o2
            for pr in range(3):
                ch = od[:, pr * LANE:(pr + 1) * LANE]
                o_ref[0, :, pr * LANE:(pr + 1) * LANE] = (
                    ch * _seg_rsqrt(ch, b64_ref, V_DIFF) * sub_ref[...] * post)
        else:
            o_ref[0] = jnp.sum(o * pick_ref[0], axis=0, keepdims=True)


def _paged_attn(pt_flat, scal, qmaps, k_pages, v_pages, bias3, knew, vnew, biasnew, pick, sub2, b64,
                layer, pool, ns, npg, diff):
    nmap = qmaps.shape[1]
    group = _page_group(npg)
    nstep = npg // group
    per = lambda n, j, pt, sc: (n, 0, 0)
    c2 = lambda n, j, pt, sc: (0, 0)
    c3 = lambda n, j, pt, sc: (0, 0, 0)
    pages = _page_specs((1, PAGE_SIZE, W3), group, layer, pool, npg)
    return pl.pallas_call(
        functools.partial(_paged_attn_kernel, group=group, diff=diff),
        grid_spec=pltpu.PrefetchScalarGridSpec(
            num_scalar_prefetch=2, grid=(ns, nstep),
            in_specs=[pl.BlockSpec((1, nmap, W3), per)] + pages + pages
                     + [pl.BlockSpec((1, nstep, group * PAGE_SIZE), per),
                        pl.BlockSpec((1, 1, W3), per), pl.BlockSpec((1, 1, W3), per),
                        pl.BlockSpec((1, 1, LANE), per),
                        pl.BlockSpec((2, nmap, W3), c3), pl.BlockSpec((1, LANE), c2), pl.BlockSpec((LANE, LANE), c2)],
            out_specs=pl.BlockSpec((1, 1, W3), per),
            scratch_shapes=[pltpu.VMEM((nmap, 1), F32), pltpu.VMEM((nmap, 1), F32), pltpu.VMEM((nmap, W3), F32)]),
        out_shape=jax.ShapeDtypeStruct((ns, 1, W3), F32),
        compiler_params=_cparams(("parallel", "arbitrary")),
    )(pt_flat, scal, qmaps, *([k_pages] * group), *([v_pages] * group), bias3, knew, vnew, biasnew, pick, sub2, b64)


def _outproj_kernel(x_ref, a_ref, d_ref, c_ref, wa_ref, wd_ref, wc_ref, g_ref, wg_ref, wu_ref,
                    x1_ref, gate_ref, up_ref):
    x1 = (x_ref[...] + _dot(a_ref[...].astype(BF16), wa_ref[...]) + _dot(d_ref[...].astype(BF16), wd_ref[...])
          + _dot(c_ref[...].astype(BF16), wc_ref[...]))
    x1_ref[...] = x1
    h = (x1 * lax.rsqrt(jnp.mean(x1 * x1, axis=-1, keepdims=True) + EPS) * g_ref[...]).astype(BF16)
    gate_ref[...] = _dot(h, wg_ref[...])
    up_ref[...] = _dot(h, wu_ref[...])


def _outproj(x2d, a, d, c, wa, wd_, wc, gffn, wg, wu, tm):
    rows, dm = x2d.shape
    dff = wg.shape[1]
    row = lambda i: (i, 0)
    const = lambda i: (0, 0)
    return pl.pallas_call(
        _outproj_kernel, grid=(rows // tm,),
        in_specs=[pl.BlockSpec((tm, dm), row)] + [pl.BlockSpec((tm, W3), row)] * 3
                 + [pl.BlockSpec((W3, dm), const)] * 3
                 + [pl.BlockSpec((1, dm), const), pl.BlockSpec((dm, dff), const), pl.BlockSpec((dm, dff), const)],
        out_specs=[pl.BlockSpec((tm, dm), row), pl.BlockSpec((tm, dff), row), pl.BlockSpec((tm, dff), row)],
        out_shape=[jax.ShapeDtypeStruct((rows, dm), F32), jax.ShapeDtypeStruct((rows, dff), F32),
                   jax.ShapeDtypeStruct((rows, dff), F32)],
        compiler_params=_cparams(("parallel",)),
    )(x2d, a, d, c, wa, wd_, wc, gffn, wg, wu)


def _ffn_tail(gm2, gm1, g0, u, x1, cw_ref, cb_ref, wd_ref):
    gc = cw_ref[0:1, :] * gm2 + cw_ref[1:2, :] * gm1 + cw_ref[2:3, :] * g0 + cb_ref[...]
    return x1 + _dot((_silu(gc) * u).astype(BF16), wd_ref[...])


def _ffn_prompt_kernel(g_ref, halo_ref, u_ref, x1_ref, cw_ref, cb_ref, wd_ref, o_ref, gs_ref):
    i = pl.program_id(1)
    tm = g_ref.shape[0]
    halo = halo_ref[...]
    gs_ref[0:8, :] = jnp.where(i == 0, jnp.zeros_like(halo), halo)
    gs_ref[8:8 + tm, :] = g_ref[...]
    o_ref[...] = _ffn_tail(gs_ref[6:6 + tm, :], gs_ref[7:7 + tm, :], g_ref[...], u_ref[...], x1_ref[...],
                           cw_ref, cb_ref, wd_ref)


def _ffn_prompt(gate, up, x1, cw, cb, wd, n, tpad, tm):
    dff, dm = wd.shape
    nb = tpad // tm
    row = lambda b, i: (b * nb + i, 0)
    halo = lambda b, i: (jnp.maximum((b * nb + i) * (tm // 8) - 1, 0), 0)
    const = lambda b, i: (0, 0)
    return pl.pallas_call(
        _ffn_prompt_kernel, grid=(n, nb),
        in_specs=[pl.BlockSpec((tm, dff), row), pl.BlockSpec((8, dff), halo), pl.BlockSpec((tm, dff), row),
                  pl.BlockSpec((tm, dm), row), pl.BlockSpec((CONV_W, dff), const), pl.BlockSpec((1, dff), const),
                  pl.BlockSpec((dff, dm), const)],
        out_specs=pl.BlockSpec((tm, dm), row),
        out_shape=jax.ShapeDtypeStruct((n * tpad, dm), F32),
        scratch_shapes=[pltpu.VMEM((tm + 8, dff), F32)],
        compiler_params=_cparams(("parallel", "arbitrary")),
    )(gate, gate, up, x1, cw, cb, wd)


def _ffn_sample_kernel(g_ref, s0_ref, s1_ref, u_ref, x1_ref, cw_ref, cb_ref, wd_ref, o_ref):
    o_ref[...] = _ffn_tail(s0_ref[...], s1_ref[...], g_ref[...], u_ref[...], x1_ref[...], cw_ref, cb_ref, wd_ref)


def _ffn_sample(gate, s0, s1, up, x1, cw, cb, wd):
    rows, dff = gate.shape
    dm = wd.shape[1]
    full = lambda shape: pl.BlockSpec(shape, lambda i: (0, 0))
    return pl.pallas_call(
        _ffn_sample_kernel, grid=(1,),
        in_specs=[full((rows, dff))] * 4 + [full((rows, dm)), full((CONV_W, dff)), full((1, dff)), full((dff, dm))],
        out_specs=full((rows, dm)),
        out_shape=jax.ShapeDtypeStruct((rows, dm), F32),
        compiler_params=_cparams(("arbitrary",)),
    )(gate, s0, s1, up, x1, cw, cb, wd)


def _rope_tables(pos, head_dim, rot_dim, theta):
    half = rot_dim // 2
    inv = jnp.exp(-math.log(theta) * jnp.arange(half, dtype=F32) / half)
    ang = pos[:, None] * inv[None, :]
    cos, sin = jnp.cos(ang), jnp.sin(ang)
    t = pos.shape[0]
    rest = head_dim - rot_dim
    c = jnp.concatenate([cos, cos, jnp.ones((t, rest), F32)], axis=1)
    s1 = jnp.concatenate([-sin, jnp.zeros((t, half + rest), F32)], axis=1)
    s2 = jnp.concatenate([jnp.zeros((t, half), F32), sin, jnp.zeros((t, rest), F32)], axis=1)
    rep = LANE // head_dim
    return [jnp.tile(a, (1, rep)) for a in (c, s1, s2)]


def _all_tables(pos):
    return (_rope_tables(pos, D_RET, D_RET, RET_THETA)
            + _rope_tables(pos, D_DSA, D_DSA // ROT_FRAC, ROPE_THETA)
            + _rope_tables(pos, D_DIFF, D_DIFF // ROT_FRAC, ROPE_THETA))


def _block_ones(width):
    i = jnp.arange(LANE)
    return (i[:, None] // width == i[None, :] // width).astype(BF16)


def _pad_cols(a, width):
    return jnp.pad(a, [(0, 0)] * (a.ndim - 1) + [(0, width - a.shape[-1])])


def _prep_w_in(w_in):
    parts, off = [], 0
    for size, width in zip(PROJ_SIZES, GROUP_WIDTHS):
        parts.append(_pad_cols(w_in[..., off:off + size], width))
        off += size
    return jnp.concatenate(parts, axis=-1).astype(BF16)


def _prep_w_out(w_out):
    a = w_out[:, :H_RET * D_RET]
    d = jnp.pad(w_out[:, H_RET * D_RET:H_RET * D_RET + MIXW], ((0, 0), (0, W3 - MIXW), (0, 0)))
    c = jnp.pad(w_out[:, H_RET * D_RET + MIXW:], ((0, 0), (0, W3 - MIXW), (0, 0)))
    return a.astype(BF16), d.astype(BF16), c.astype(BF16)


def _tile_gain(g):
    return jnp.tile(g, (1, LANE // g.shape[-1]))[:, None, :]


def kernel(x_prompt, x_sample, cache_dsa_k, cache_dsa_v, cache_dsa_kidx, cache_diff_k, cache_diff_v, state_ret, state_conv, page_table, meta, norm_mix_g, w_in, ret_gn_g, dsa_qn_g, dsa_kn_g, diff_qn_g, diff_kn_g, diff_lq1, diff_lk1, diff_lq2, diff_lk2, diff_subln_g, w_out, norm_ffn_g, ffn_w_gate, ffn_w_up, ffn_conv_w, ffn_conv_b, ffn_w_down):
    nb, seq, dm = x_prompt.shape
    ns, ts, _ = x_sample.shape
    assert ts == 1
    depth = w_in.shape[0]
    dff = ffn_w_gate.shape[-1]
    pool = cache_dsa_k.shape[1]
    npg = page_table.shape[1]
    past = npg * PAGE_SIZE
    tp = seq + N_META
    tpad = -(-tp // K_CHUNK) * K_CHUNK
    k_p = min(TOPK_MAX, tp // 4)
    k_s = min(TOPK_MAX, (past + ts) // 4)
    tm_p = ROW_TILE
    tm_s = ns
    assert ns % SAMPLE_TILE == 0 and ns % 8 == 0

    w_in_p = _prep_w_in(w_in)
    wo_a, wo_d, wo_c = _prep_w_out(w_out)
    wg16, wu16, wd16 = ffn_w_gate.astype(BF16), ffn_w_up.astype(BF16), ffn_w_down.astype(BF16)
    gmix = norm_mix_g[:, None, :]
    gffn = norm_ffn_g[:, None, :]
    qn2, kn2 = _tile_gain(dsa_qn_g), _tile_gain(dsa_kn_g)
    fqn4, fkn4 = _tile_gain(diff_qn_g), _tile_gain(diff_kn_g)
    sub2 = _tile_gain(diff_subln_g)
    gn_flat = ret_gn_g[:, None, :]
    gn_heads = ret_gn_g.reshape(depth, H_RET, D_RET)
    b64, b32 = _block_ones(64), _block_ones(32)
    tabs_p = _all_tables(jnp.arange(tpad, dtype=F32))
    tabs_s = _all_tables(jnp.full((ns,), float(past), F32))
    cb = ffn_conv_b[:, None, :]

    lam_all = (jnp.exp(jnp.sum(diff_lq1 * diff_lk1, axis=-1)) - jnp.exp(jnp.sum(diff_lq2 * diff_lk2, axis=-1)))

    lane_head = jnp.arange(W3) // V_DIFF
    rows16 = jnp.arange(16)
    pick_dsa = jnp.stack([(rows16[:, None] == lane_head[None, :]), jnp.zeros((16, W3), bool)]).astype(F32)
    pick_diff = jnp.stack([(rows16[:, None] == 2 * lane_head[None, :]),
                           (rows16[:, None] == 2 * lane_head[None, :] + 1)]).astype(F32)

    kidx_pages = cache_dsa_kidx.reshape(depth * pool, PAGE_SIZE, D_IDX)
    def mxu_pages(cache):
        heads = jnp.pad(cache.astype(BF16), ((0, 0), (0, 0), (0, 0), (0, W3 // D_DSA - H_DSA), (0, 0)))
        return heads.reshape(depth * pool, PAGE_SIZE, W3)

    dsak_pages, dsav_pages = mxu_pages(cache_dsa_k), mxu_pages(cache_dsa_v)
    difk_pages, difv_pages = mxu_pages(cache_diff_k), mxu_pages(cache_diff_v)
    state4 = state_ret.reshape(depth * ns, H_RET, D_RET, D_RET)
    pt_flat = page_table.reshape(-1).astype(I32)

    xp = jnp.concatenate([jnp.broadcast_to(meta.astype(x_prompt.dtype), (nb, N_META, dm)), x_prompt], axis=1)
    xp = jnp.pad(xp, ((0, 0), (0, tpad - tp), (0, 0))).reshape(nb * tpad, dm)
    xs = x_sample.reshape(ns, dm)

    outs_p = [[] for _ in range(7)]
    outs_s = [[] for _ in range(7)]
    for l in range(depth):
        lam_init = 0.8 - 0.6 * math.exp(-0.3 * l)
        scal = jnp.stack([lam_all[l] + lam_init, jnp.asarray(1.0 - lam_init, F32)]).astype(F32)
        gains = (qn2[l], kn2[l], fqn4[l], fkn4[l])

        (ret4, bq16, bk16, bv16, iq16, ik16, iw, cq16, ck16, cv16, bk32, bv32, ik32, ck32, cv32) = _inproj(
            xp, gmix[l], w_in_p[l], tabs_p, gains, (b64, b32), tm_p)
        a_p, s_p = _ret_prompt(ret4, gn_flat[l], b64, nb, tpad, tp)
        d_p = _dsa_prompt(bq16, iq16, iw, ik16, bk16, bv16, nb, tpad, tp, k_p)
        c_p = _diff_prompt(scal, cq16, ck16, cv16, sub2[l], b64, nb, tpad, tp)
        x1, gate, up = _outproj(xp, a_p, d_p, c_p, wo_a[l], wo_d[l], wo_c[l], gffn[l], wg16[l], wu16[l], tm_p)
        xp = _ffn_prompt(gate, up, x1, ffn_conv_w[l], cb[l], wd16[l], nb, tpad, tm_p)

        def rows_p(a):
            return a.reshape(nb, tpad, -1)[:, :tp]

        outs_p[0].append(s_p)
        outs_p[1].append(gate.reshape(nb, tpad, dff)[:, tp - (CONV_W - 1):tp])
        outs_p[2].append(rows_p(bk32).reshape(nb, tp, H_DSA, D_DSA))
        outs_p[3].append(rows_p(bv32).reshape(nb, tp, H_DSA, D_DSA))
        outs_p[4].append(rows_p(ik32))
        outs_p[5].append(rows_p(ck32).reshape(nb, tp, H_DIFF, 2 * D_DIFF))
        outs_p[6].append(rows_p(cv32).reshape(nb, tp, H_DIFF, V_DIFF))

        (ret4, bq16, bk16, bv16, iq16, ik16, iw, cq16, ck16, cv16, bk32, bv32, ik32, ck32, cv32) = _inproj(
            xs, gmix[l], w_in_p[l], tabs_s, gains, (b64, b32), tm_s)
        nblk = ns // SAMPLE_TILE
        to_cols = lambda a: a.reshape(nblk, SAMPLE_TILE, W3).transpose(0, 2, 1)
        a3, s_s = _ret_sample(to_cols(ret4[:, O_RQ:O_RQ + W3]), to_cols(ret4[:, O_RK:O_RK + W3]),
                              ret4[:, O_RV:O_RV + W3].reshape(ns, H_RET, D_RET),
                              ret4[:, O_RG:O_RG + W3].reshape(ns, H_RET, D_RET), gn_heads[l], state4, l, ns)
        a_s = a3.reshape(ns, W3)
        iq3 = jnp.pad(iq16.reshape(ns, H_IDX, D_IDX), ((0, 0), (0, 16 - H_IDX), (0, 0)))
        iwb = jnp.pad(jnp.broadcast_to(iw[:, :H_IDX, None], (ns, H_IDX, LANE)), ((0, 0), (0, 16 - H_IDX), (0, 0)))
        sc_pages, sc_new = _idx_sample(pt_flat, iq3, iwb, kidx_pages, ik32[:, None, :], l, pool, ns, npg)
        n_keys = past + ts
        kc_s = K_CHUNK
        nkc_s = -(-n_keys // kc_s)
        scores = jnp.concatenate([sc_pages.reshape(ns, past), sc_new[:, 0, :1],
                                  jnp.full((ns, nkc_s * kc_s - n_keys), NEG, F32)], axis=1)
        bias = _select_sample(scores.reshape(ns, nkc_s, kc_s).transpose(1, 0, 2), k_s)
        bias = bias.transpose(1, 0, 2).reshape(ns, nkc_s * kc_s)
        grp = _page_group(npg)
        bias_pages = bias[:, :past].reshape(ns, npg // grp, grp * PAGE_SIZE)
        bias_new = jnp.broadcast_to(bias[:, past:past + 1, None], (ns, 1, LANE))
        q_dsa = jnp.where(pick_dsa[0][None] > 0, bq16[:, None, :], 0).astype(BF16)
        pad8 = lambda a: _pad_cols(a, W3)[:, None, :]
        d_s = _paged_attn(pt_flat, scal, q_dsa, dsak_pages, dsav_pages, bias_pages, pad8(bk32), pad8(bv32), bias_new,
                          pick_dsa, sub2[l], b64, l, pool, ns, npg, False).reshape(ns, W3)
        lane_sub = (jnp.arange(W3) % V_DIFF) // D_DIFF
        pick_q = (rows16[:, None] == (2 * lane_head + lane_sub)[None, :])
        q_dif = jnp.where(pick_q[None], cq16[:, None, :], 0).astype(BF16)
        zero_pages = jnp.zeros_like(bias_pages)
        zero_new = jnp.zeros((ns, 1, LANE), F32)
        c_s = _paged_attn(pt_flat, scal, q_dif, difk_pages, difv_pages, zero_pages, pad8(ck32), pad8(cv32), zero_new,
                          pick_diff, sub2[l], b64, l, pool, ns, npg, True).reshape(ns, W3)
        x1, gate, up = _outproj(xs, a_s, d_s, c_s, wo_a[l], wo_d[l], wo_c[l], gffn[l], wg16[l], wu16[l], tm_s)
        xs = _ffn_sample(gate, state_conv[l, :, 0], state_conv[l, :, 1], up, x1, ffn_conv_w[l], cb[l], wd16[l])

        outs_s[0].append(s_s)
        outs_s[1].append(jnp.stack([state_conv[l, :, 1], gate], axis=1))
        outs_s[2].append(bk32.reshape(ns, ts, H_DSA, D_DSA))
        outs_s[3].append(bv32.reshape(ns, ts, H_DSA, D_DSA))
        outs_s[4].append(ik32.reshape(ns, ts, D_IDX))
        outs_s[5].append(ck32.reshape(ns, ts, H_DIFF, 2 * D_DIFF))
        outs_s[6].append(cv32.reshape(ns, ts, H_DIFF, V_DIFF))

    ret_p, conv_p, dsak_p, dsav_p, kidx_p, diffk_p, diffv_p = (jnp.stack(a) for a in outs_p)
    ret_s, conv_s, dsak_s, dsav_s, kidx_s, diffk_s, diffv_s = (jnp.stack(a) for a in outs_s)
    y_prompt = xp.reshape(nb, tpad, dm)[:, N_META:tp]
    y_sample = xs.reshape(ns, ts, dm)
    return (y_prompt, y_sample, ret_p, ret_s, conv_p, conv_s, dsak_p, dsak_s, dsav_p, dsav_s,
            kidx_p, kidx_s, diffk_p, diffk_s, diffv_p, diffv_s)
```
